```python
import math
import jax, jax.numpy as jnp
from jax import lax
import numpy as np

D_MODEL = 1024
BATCH = 4
SEQ = 8192
DEPTH = 1

D_MIX = D_MODEL
D_ATTN = D_MIX // 2
D_POOL = D_MIX - D_ATTN
HEAD_DIM = 64
ATTN_HEADS = D_ATTN // HEAD_DIM
DILATED_PATTERNS = ((128, 1), (512, 4), (2048, 16))
WBLK = 128
POOL_WINDOWS = (2, 4, 8, 16)
N_POOL_GROUPS = len(POOL_WINDOWS)
POOL_GROUP_DIM = D_POOL // N_POOL_GROUPS
PEER_HEADS = 8
N_KEYS = 128
N_EXPERTS = N_KEYS * N_KEYS
PEER_TOPK = 16
PEER_QDIM = 256
PEER_HALF = PEER_QDIM // 2
TOKEN_CHUNK = 128
EPS = 1e-6
NEG = -1e30

kernel_name = "hymba_dilated_pool_peer_block"


def rmsnorm(x, g):
    xf = x.astype(jnp.float32)
    y = xf * lax.rsqrt(jnp.mean(xf * xf, axis=-1, keepdims=True) + EPS)
    return (y * g.astype(jnp.float32)).astype(x.dtype)


def dilated_branch(q, k, v, window, dilation):
    b, s, h, dh = q.shape
    n_steps = window // dilation
    L = s // dilation
    nb = -(-L // WBLK)
    Lp = nb * WBLK

    def to_sub(t):
        t = t.reshape(b, L, dilation, h, dh).transpose(0, 2, 3, 1, 4)
        t = jnp.pad(t, ((0, 0), (0, 0), (0, 0), (0, Lp - L), (0, 0)))
        return t.reshape(b, dilation, h, nb, WBLK, dh)

    def with_prev(t):
        prev = jnp.pad(t, ((0, 0), (0, 0), (0, 0), (1, 0), (0, 0), (0, 0)))[:, :, :, :nb]
        return jnp.concatenate([prev, t], axis=4)

    qb = to_sub(q)
    kx = with_prev(to_sub(k))
    vx = with_prev(to_sub(v))
    scores = jnp.einsum('brhnqd,brhnkd->brhnqk', qb, kx).astype(jnp.float32) * (dh ** -0.5)
    qi = jnp.arange(WBLK)[:, None]
    kj = jnp.arange(2 * WBLK)[None, :]
    dist = qi + WBLK - kj
    kpos = jnp.arange(nb)[:, None, None] * WBLK - WBLK + kj[None]
    mask = (dist >= 0)[None] & (dist <= n_steps)[None] & (kpos >= 0)
    scores = jnp.where(mask[None, None, None], scores, NEG)
    m = jnp.max(scores, axis=-1, keepdims=True)
    p = jnp.exp(scores - m)
    den = jnp.sum(p, axis=-1, keepdims=True)
    o = jnp.einsum('brhnqk,brhnkd->brhnqd', p, vx.astype(jnp.float32)) / den
    lse = (m + jnp.log(den))[..., 0]
    o = o.reshape(b, dilation, h, Lp, dh)[:, :, :, :L].transpose(0, 3, 1, 2, 4).reshape(b, s, h, dh)
    lse = lse.reshape(b, dilation, h, Lp)[..., :L].transpose(0, 3, 1, 2).reshape(b, s, h)
    return o, lse


def dilated_attention(q, k, v):
    outs, lses = [], []
    for window, dilation in DILATED_PATTERNS:
        o, lse = dilated_branch(q, k, v, window, dilation)
        outs.append(o)
        lses.append(lse)
    w = jax.nn.softmax(jnp.stack(lses, axis=0), axis=0)
    o = jnp.sum(w[..., None] * jnp.stack(outs, axis=0), axis=0)
    return o.astype(q.dtype)


def pool_mixer(z, w_pool, pool_scale):
    b, s, c = z.shape
    zg = z.astype(jnp.float32).reshape(b, s, N_POOL_GROUPS, POOL_GROUP_DIM)
    cs = jnp.cumsum(zg, axis=1)
    t = jnp.arange(1, s + 1, dtype=jnp.float32)
    outs = []
    for g, win in enumerate(POOL_WINDOWS):
        csg = cs[:, :, g]
        lag = jnp.pad(csg, ((0, 0), (win, 0), (0, 0)))[:, :s]
        mean = (csg - lag) / jnp.minimum(t, float(win))[None, :, None]
        outs.append(mean - zg[:, :, g])
    pooled = jnp.stack(outs, axis=2)
    mixed = jnp.einsum('bsgc,gcd->bsgd', pooled, w_pool.astype(jnp.float32)).reshape(b, s, c)
    return (mixed * pool_scale.astype(jnp.float32)).astype(z.dtype)


def peer(h, w_query, sub_keys, expert_u, expert_v):
    b, s, d = h.shape
    q = jnp.einsum('bsd,de->bse', h, w_query).reshape(b, s, PEER_HEADS, 2, PEER_HALF)
    sc = jnp.einsum('bshpc,hpkc->bshpk', q, sub_keys).astype(jnp.float32)
    top_s, top_i = lax.top_k(sc, PEER_TOPK)
    cand_s = top_s[..., 0, :, None] + top_s[..., 1, None, :]
    cand_i = top_i[..., 0, :, None] * N_KEYS + top_i[..., 1, None, :]
    kk = PEER_TOPK * PEER_TOPK
    best_s, best_pos = lax.top_k(cand_s.reshape(b, s, PEER_HEADS, kk), PEER_TOPK)
    expert_idx = jnp.take_along_axis(cand_i.reshape(b, s, PEER_HEADS, kk), best_pos, axis=-1)
    gates = jax.nn.softmax(best_s, axis=-1)
    n_chunks = (b * s) // TOKEN_CHUNK
    xc = h.reshape(n_chunks, TOKEN_CHUNK, d)
    ic = expert_idx.reshape(n_chunks, TOKEN_CHUNK, PEER_HEADS, PEER_TOPK)
    gc = gates.reshape(n_chunks, TOKEN_CHUNK, PEER_HEADS, PEER_TOPK)

    def chunk_fn(args):
        xt, it, gt = args
        u = expert_u[it]
        a = jnp.einsum('chkd,cd->chk', u, xt).astype(jnp.float32)
        act = jax.nn.gelu(a, approximate=False) * gt
        vsel = expert_v[it]
        return jnp.einsum('chk,chkd->cd', act, vsel.astype(jnp.float32))

    y = lax.map(chunk_fn, (xc, ic, gc))
    return y.reshape(b, s, d).astype(h.dtype)


def setup_inputs(seed: int = 0) -> dict:
    key = jax.random.key(seed)
    ks = jax.random.split(key, 13)
    f32 = jnp.float32
    x = jax.random.normal(ks[0], (BATCH, SEQ, D_MODEL), f32)
    norm_mix = 1.0 + 0.1 * jax.random.normal(ks[1], (DEPTH, D_MODEL), f32)
    w_in = jax.random.normal(ks[2], (DEPTH, D_MODEL, 3 * D_ATTN + D_POOL), f32) * D_MODEL ** -0.5
    w_pool = jax.random.normal(ks[3], (DEPTH, N_POOL_GROUPS, POOL_GROUP_DIM, POOL_GROUP_DIM), f32) * POOL_GROUP_DIM ** -0.5
    pool_scale = 1.0 + 0.1 * jax.random.normal(ks[4], (DEPTH, D_POOL), f32)
    w_out = jax.random.normal(ks[5], (DEPTH, D_MIX, D_MODEL), f32) * D_MIX ** -0.5
    norm_ffn = 1.0 + 0.1 * jax.random.normal(ks[6], (DEPTH, D_MODEL), f32)
    w_query = jax.random.normal(ks[7], (DEPTH, D_MODEL, PEER_HEADS * PEER_QDIM), f32) * D_MODEL ** -0.5
    sub_keys = jax.random.normal(ks[8], (DEPTH, PEER_HEADS, 2, N_KEYS, PEER_HALF), f32) * PEER_HALF ** -0.5
    expert_u = jax.random.normal(ks[9], (DEPTH, N_EXPERTS, D_MODEL), f32) * D_MODEL ** -0.5
    expert_v = jax.random.normal(ks[10], (DEPTH, N_EXPERTS, D_MODEL), f32) * PEER_TOPK ** -0.5
    norm_final = 1.0 + 0.1 * jax.random.normal(ks[11], (D_MODEL,), f32)
    return {"x": x, "norm_mix": norm_mix, "w_in": w_in, "w_pool": w_pool, "pool_scale": pool_scale,
            "w_out": w_out, "norm_ffn": norm_ffn, "w_query": w_query, "sub_keys": sub_keys,
            "expert_u": expert_u, "expert_v": expert_v, "norm_final": norm_final}


def reference(x, norm_mix, w_in, w_pool, pool_scale, w_out, norm_ffn, w_query, sub_keys,
              expert_u, expert_v, norm_final):
    b, s, _ = x.shape
    for layer in range(DEPTH):
        h = rmsnorm(x, norm_mix[layer])
        z = jnp.einsum('bsd,de->bse', h, w_in[layer])
        q = z[..., :D_ATTN].reshape(b, s, ATTN_HEADS, HEAD_DIM)
        k = z[..., D_ATTN:2 * D_ATTN].reshape(b, s, ATTN_HEADS, HEAD_DIM)
        v = z[..., 2 * D_ATTN:3 * D_ATTN].reshape(b, s, ATTN_HEADS, HEAD_DIM)
        zp = z[..., 3 * D_ATTN:]
        att = dilated_attention(q, k, v).reshape(b, s, D_ATTN)
        pooled = pool_mixer(zp, w_pool[layer], pool_scale[layer])
        mix = jnp.concatenate([att, pooled], axis=-1)
        x = x + jnp.einsum('bse,ed->bsd', mix, w_out[layer])
        hf = rmsnorm(x, norm_ffn[layer])
        x = x + peer(hf, w_query[layer], sub_keys[layer], expert_u[layer], expert_v[layer])
    return rmsnorm(x, norm_final)
```

```python
import functools

import jax
import jax.numpy as jnp
from jax import lax
from jax.experimental import pallas as pl
from jax.experimental.pallas import tpu as pltpu

F32 = jnp.float32
BF16 = jnp.bfloat16

EPS = 1e-6
NEG = -1e30
HEAD_DIM = 64
LANES = 128
DILATED_PATTERNS = ((128, 1), (512, 4), (2048, 16))
WBLK = 128
POOL_WINDOWS = (2, 4, 8, 16)
POOL_HALO = 16
PEER_HEADS = 8
N_KEYS = 128
PEER_TOPK = 16
VMEM_LIMIT = 56 * 1024 * 1024


def _cparams(sem):
    return pltpu.CompilerParams(dimension_semantics=sem, vmem_limit_bytes=VMEM_LIMIT)


def _rms(x, g):
    ms = jnp.mean(x * x, axis=-1, keepdims=True)
    return x * lax.rsqrt(ms + EPS) * g


def _inproj_kernel(x_ref, g_ref, w_ref, qkv_ref, zp_ref, *, d_qkv):
    h = _rms(x_ref[...], g_ref[...]).astype(BF16)
    z = jnp.dot(h, w_ref[...], preferred_element_type=F32)
    qkv_ref[...] = z[:, :d_qkv].astype(BF16)
    zp_ref[...] = z[:, d_qkv:]


def _in_proj(x2, g, w_bf, d_qkv, tm):
    t, d = x2.shape
    e = w_bf.shape[1]
    return pl.pallas_call(
        functools.partial(_inproj_kernel, d_qkv=d_qkv),
        grid=(t // tm,),
        in_specs=[pl.BlockSpec((tm, d), lambda i: (i, 0)),
                  pl.BlockSpec((1, d), lambda i: (0, 0)),
                  pl.BlockSpec((d, e), lambda i: (0, 0))],
        out_specs=[pl.BlockSpec((tm, d_qkv), lambda i: (i, 0)),
                   pl.BlockSpec((tm, e - d_qkv), lambda i: (i, 0))],
        out_shape=[jax.ShapeDtypeStruct((t, d_qkv), BF16),
                   jax.ShapeDtypeStruct((t, e - d_qkv), F32)],
        compiler_params=_cparams(("parallel",)),
        name="in_proj",
    )(x2, g, w_bf)


def _attn_kernel(q_ref, kp_ref, kc_ref, vp_ref, vc_ref, o_ref, lse_ref, *, n_steps):
    n = pl.program_id(3)
    q = q_ref[...]
    kk = jnp.concatenate([kp_ref[...], kc_ref[...]], axis=0)
    vv = jnp.concatenate([vp_ref[...], vc_ref[...]], axis=0)
    lane_q = lax.broadcasted_iota(jnp.int32, q.shape, 1) < HEAD_DIM
    lane_v = lax.broadcasted_iota(jnp.int32, vv.shape, 1) < HEAD_DIM
    qi = lax.broadcasted_iota(jnp.int32, (WBLK, 2 * WBLK), 0)
    kj = lax.broadcasted_iota(jnp.int32, (WBLK, 2 * WBLK), 1)
    dist = qi + WBLK - kj
    mask = (dist >= 0) & (dist <= n_steps) & ((kj >= WBLK) | (n > 0))
    one = jnp.ones((), vv.dtype)
    zero = jnp.zeros((), q.dtype)

    def head(first):
        sel_q = lane_q if first else jnp.logical_not(lane_q)
        sel_v = lane_v if first else jnp.logical_not(lane_v)
        qh = jnp.where(sel_q, q, zero)
        s = lax.dot_general(qh, kk, (((1,), (1,)), ((), ())), preferred_element_type=F32)
        s = jnp.where(mask, s * (HEAD_DIM ** -0.5), NEG)
        m = jnp.max(s, axis=-1, keepdims=True)
        p = jnp.exp(s - m).astype(BF16)
        pv = jnp.dot(p, jnp.where(sel_v, vv, one), preferred_element_type=F32)
        den = pltpu.roll(pv, HEAD_DIM, 1)
        return pv / den, m + jnp.log(den)

    oa, la = head(True)
    ob, lb = head(False)
    out_lane = lax.broadcasted_iota(jnp.int32, oa.shape, 1) < HEAD_DIM
    o_ref[...] = jnp.where(out_lane, oa, ob)
    lse_ref[...] = jnp.where(out_lane, la, lb)


def _attention_branch(qkv6, n_steps):
    _, b, r, hp, l, _ = qkv6.shape
    nb = l // WBLK
    blk = (None, None, None, None, WBLK, LANES)

    def spec(which, prev):
        if prev:
            return pl.BlockSpec(blk, lambda bi, ri, hi, ni: (which, bi, ri, hi, jnp.maximum(ni - 1, 0), 0))
        return pl.BlockSpec(blk, lambda bi, ri, hi, ni: (which, bi, ri, hi, ni, 0))

    oblk = pl.BlockSpec((None, None, None, WBLK, LANES), lambda bi, ri, hi, ni: (bi, ri, hi, ni, 0))
    shape = jax.ShapeDtypeStruct((b, r, hp, l, LANES), F32)
    return pl.pallas_call(
        functools.partial(_attn_kernel, n_steps=n_steps),
        grid=(b, r, hp, nb),
        in_specs=[spec(0, False), spec(1, True), spec(1, False), spec(2, True), spec(2, False)],
        out_specs=[oblk, oblk],
        out_shape=[shape, shape],
        compiler_params=_cparams(("parallel", "parallel", "parallel", "arbitrary")),
        name=f"attn_r{r}",
    )(qkv6, qkv6, qkv6, qkv6, qkv6)


def _mixout_kernel(o1_ref, o2_ref, o3_ref, l1_ref, l2_ref, l3_ref, zp_ref, zh_ref, x_ref,
                   wp_ref, ps_ref, wo_ref, g_ref, x1_ref, hf_ref, hft_ref, *, tiles_per_seq):
    i = pl.program_id(0)
    tm = x_ref.shape[0]
    first = (i % tiles_per_seq) == 0

    l1, l2, l3 = l1_ref[...], l2_ref[...], l3_ref[...]
    lm = jnp.maximum(jnp.maximum(l1, l2), l3)
    e1, e2, e3 = jnp.exp(l1 - lm), jnp.exp(l2 - lm), jnp.exp(l3 - lm)
    att = (e1 * o1_ref[...] + e2 * o2_ref[...] + e3 * o3_ref[...]) / (e1 + e2 + e3)

    zc = zp_ref[...]
    halo = jnp.where(first, 0.0, zh_ref[...])
    buf = jnp.concatenate([zc, halo], axis=0)
    sums = {1: buf}
    w = 1
    while w < POOL_WINDOWS[-1]:
        sums[2 * w] = sums[w] + pltpu.roll(sums[w], w, 0)
        w *= 2
    pos = (i % tiles_per_seq) * tm + lax.broadcasted_iota(jnp.int32, (tm, 1), 0) + 1
    cg = zc.shape[1] // len(POOL_WINDOWS)
    mixed = []
    for g, win in enumerate(POOL_WINDOWS):
        cols = slice(g * cg, (g + 1) * cg)
        cnt = jnp.minimum(pos, win).astype(F32)
        pooled = sums[win][:tm, cols] / cnt - zc[:, cols]
        mixed.append(jnp.dot(pooled.astype(BF16), wp_ref[g], preferred_element_type=F32))
    mixed = jnp.concatenate(mixed, axis=1) * ps_ref[...]

    mix = jnp.concatenate([att, mixed], axis=1).astype(BF16)
    x1 = x_ref[...] + jnp.dot(mix, wo_ref[...], preferred_element_type=F32)
    x1_ref[...] = x1
    hf = _rms(x1, g_ref[...])
    hf_ref[...] = hf.astype(BF16)
    hft_ref[...] = hf.T.astype(BF16)


def _mix_out(o_list, l_list, zp, x2, wp_bf, ps, wo_bf, g, seq, tm):
    t, d = x2.shape
    da = o_list[0].shape[1]
    dp = zp.shape[1]
    row = lambda i: (i, 0)
    const2 = lambda i: (0, 0)
    halo_rows = tm // POOL_HALO
    return pl.pallas_call(
        functools.partial(_mixout_kernel, tiles_per_seq=seq // tm),
        grid=(t // tm,),
        in_specs=[pl.BlockSpec((tm, da), row)] * 6 + [
            pl.BlockSpec((tm, dp), row),
            pl.BlockSpec((POOL_HALO, dp), lambda i: (jnp.maximum(i * halo_rows - 1, 0), 0)),
            pl.BlockSpec((tm, d), row),
            pl.BlockSpec(wp_bf.shape, lambda i: (0, 0, 0)),
            pl.BlockSpec((1, dp), const2),
            pl.BlockSpec(wo_bf.shape, const2),
            pl.BlockSpec((1, d), const2)],
        out_specs=[pl.BlockSpec((tm, d), row), pl.BlockSpec((tm, d), row),
                   pl.BlockSpec((d, tm), lambda i: (0, i))],
        out_shape=[jax.ShapeDtypeStruct((t, d), F32), jax.ShapeDtypeStruct((t, d), BF16),
                   jax.ShapeDtypeStruct((d, t), BF16)],
        compiler_params=_cparams(("parallel",)),
        name="mix_out",
    )(*o_list, *l_list, zp, zp, x2, wp_bf, ps, wo_bf, g)


def _young_cells():
    return [(a, b) for a in range(PEER_TOPK) for b in range(PEER_TOPK)
            if (a + 1) * (b + 1) <= PEER_TOPK]


def _route_kernel(hf_ref, wq_ref, sk_ref, rank1_ref, e1_ref, nk_ref, e0_ref,
                  s_scr, rank_scr, val_scr, n_scr, aux_scr):
    tmr = hf_ref.shape[0]
    q = jnp.dot(hf_ref[...], wq_ref[...], preferred_element_type=F32).astype(BF16)
    for hp in range(2 * PEER_HEADS):
        s_scr[hp] = lax.dot_general(sk_ref[hp], q[:, hp * N_KEYS:(hp + 1) * N_KEYS],
                                    (((1,), (1,)), ((), ())), preferred_element_type=F32)

    key_id = lax.broadcasted_iota(jnp.int32, (N_KEYS, tmr), 0).astype(F32)

    for hp in range(2 * PEER_HEADS):
        h, p = divmod(hp, 2)

        def extract(r, carry):
            sw, rank = carry
            m = jnp.max(sw, axis=0, keepdims=True)
            idx = jnp.min(jnp.where(sw == m, key_id, float(N_KEYS)), axis=0, keepdims=True)
            sel = key_id == idx
            val_scr[p, r, pl.ds(h, 1), :] = m
            return jnp.where(sel, -jnp.inf, sw), jnp.where(sel, r.astype(F32), rank)

        _, rank = lax.fori_loop(0, PEER_TOPK, extract,
                                (s_scr[hp], jnp.full((N_KEYS, tmr), float(PEER_TOPK), F32)))
        rank_scr[hp] = rank

    cells = _young_cells()
    v0 = [val_scr[0, a] for a in range(PEER_TOPK)]
    v1 = [val_scr[1, b] for b in range(PEER_TOPK)]
    csum = {c: v0[c[0]] + v1[c[1]] for c in cells}
    beaten = {c: jnp.full(csum[c].shape, float((c[0] + 1) * (c[1] + 1) - 1), F32) for c in cells}
    for ix, cx in enumerate(cells):
        for cy in cells[ix + 1:]:
            comparable = (cx[0] <= cy[0] and cx[1] <= cy[1]) or (cy[0] <= cx[0] and cy[1] <= cx[1])
            if comparable:
                continue
            y_wins = jnp.where(csum[cy] > csum[cx], 1.0, 0.0)
            beaten[cx] = beaten[cx] + y_wins
            beaten[cy] = beaten[cy] + (1.0 - y_wins)
    top = csum[(0, 0)]
    zsum = jnp.zeros_like(top)
    ncol = [jnp.zeros_like(top) for _ in range(PEER_TOPK)]
    for c in cells:
        chosen = beaten[c] < float(PEER_TOPK)
        zsum = zsum + jnp.where(chosen, jnp.exp(csum[c] - top), 0.0)
        ncol[c[0]] = ncol[c[0]] + jnp.where(chosen, 1.0, 0.0)
    for a in range(PEER_TOPK):
        n_scr[a] = ncol[a]
    aux_scr[0] = 1.0 / zsum
    aux_scr[1] = v0[0]
    aux_scr[2] = v1[0]

    for h in range(PEER_HEADS):
        rank0 = rank_scr[2 * h]
        nk = jnp.zeros((N_KEYS, tmr), F32)
        for a in range(PEER_TOPK):
            nk = jnp.where(rank0 == float(a), n_scr[a, pl.ds(h, 1), :], nk)
        nk_ref[h] = nk
        e0_ref[h] = jnp.exp(s_scr[2 * h] - aux_scr[1, pl.ds(h, 1), :]) * aux_scr[0, pl.ds(h, 1), :]
        e1_ref[h] = jnp.exp(s_scr[2 * h + 1] - aux_scr[2, pl.ds(h, 1), :])
        rank1_ref[h] = rank_scr[2 * h + 1]


def _route(hf, wq_bf, sk_bf, tmr):
    t, d = hf.shape
    sk2 = sk_bf.reshape(2 * PEER_HEADS, N_KEYS, sk_bf.shape[-1])
    oshape = jax.ShapeDtypeStruct((PEER_HEADS, N_KEYS, t), F32)
    ospec = pl.BlockSpec((PEER_HEADS, N_KEYS, tmr), lambda i: (0, 0, i))
    return pl.pallas_call(
        _route_kernel,
        grid=(t // tmr,),
        in_specs=[pl.BlockSpec((tmr, d), lambda i: (i, 0)),
                  pl.BlockSpec(wq_bf.shape, lambda i: (0, 0)),
                  pl.BlockSpec(sk2.shape, lambda i: (0, 0, 0))],
        out_specs=[ospec] * 4,
        out_shape=[oshape] * 4,
        scratch_shapes=[pltpu.VMEM((2 * PEER_HEADS, N_KEYS, tmr), F32),
                        pltpu.VMEM((2 * PEER_HEADS, N_KEYS, tmr), F32),
                        pltpu.VMEM((2, PEER_TOPK, PEER_HEADS, tmr), F32),
                        pltpu.VMEM((PEER_TOPK, PEER_HEADS, tmr), F32),
                        pltpu.VMEM((3, PEER_HEADS, tmr), F32)],
        compiler_params=_cparams(("parallel",)),
        name="route",
    )(hf, wq_bf, sk2)


def _peer_kernel(u_ref, hft_ref, vt_ref, rank1_ref, e1_ref, nk_ref, e0_ref, yt_ref, wact_scr):
    c = pl.program_id(1)
    ec = u_ref.shape[0]
    rows_per_step = ec // N_KEYS
    s = jnp.dot(u_ref[...], hft_ref[...], preferred_element_type=F32)
    act = 0.5 * s * (1.0 + lax.erf(s * (0.5 ** 0.5)))
    for j in range(rows_per_step):
        i0 = c * rows_per_step + j
        gate = jnp.zeros((N_KEYS, s.shape[1]), F32)
        for h in range(PEER_HEADS):
            chosen = rank1_ref[h] < nk_ref[h, pl.ds(i0, 1), :]
            gate = gate + jnp.where(chosen, e1_ref[h] * e0_ref[h, pl.ds(i0, 1), :], 0.0)
        wact_scr[j * N_KEYS:(j + 1) * N_KEYS, :] = (act[j * N_KEYS:(j + 1) * N_KEYS] * gate).astype(BF16)
    part = jnp.dot(vt_ref[...], wact_scr[...], preferred_element_type=F32)

    @pl.when(c == 0)
    def _():
        yt_ref[...] = part

    @pl.when(c > 0)
    def _():
        yt_ref[...] += part


def _peer(u_bf, hft, vt_bf, rank1, e1, nk, e0, tm, ec):
    n_exp, d = u_bf.shape
    t = hft.shape[1]
    rspec = pl.BlockSpec((PEER_HEADS, N_KEYS, tm), lambda i, c: (0, 0, i))
    return pl.pallas_call(
        _peer_kernel,
        grid=(t // tm, n_exp // ec),
        in_specs=[pl.BlockSpec((ec, d), lambda i, c: (c, 0)),
                  pl.BlockSpec((d, tm), lambda i, c: (0, i)),
                  pl.BlockSpec((d, ec), lambda i, c: (0, c)),
                  rspec, rspec, rspec, rspec],
        out_specs=pl.BlockSpec((d, tm), lambda i, c: (0, i)),
        out_shape=jax.ShapeDtypeStruct((d, t), F32),
        scratch_shapes=[pltpu.VMEM((ec, tm), BF16)],
        compiler_params=_cparams(("parallel", "arbitrary")),
        name="peer",
    )(u_bf, hft, vt_bf, rank1, e1, nk, e0)


def _final_kernel(x1_ref, yt_ref, g_ref, o_ref):
    o_ref[...] = _rms(x1_ref[...] + yt_ref[...].T, g_ref[...])


def _add_peer_kernel(x1_ref, yt_ref, o_ref):
    o_ref[...] = x1_ref[...] + yt_ref[...].T


def _add_peer(x1, yt, tm):
    t, d = x1.shape
    return pl.pallas_call(
        _add_peer_kernel,
        grid=(t // tm,),
        in_specs=[pl.BlockSpec((tm, d), lambda i: (i, 0)), pl.BlockSpec((d, tm), lambda i: (0, i))],
        out_specs=pl.BlockSpec((tm, d), lambda i: (i, 0)),
        out_shape=jax.ShapeDtypeStruct((t, d), F32),
        compiler_params=_cparams(("parallel",)),
        name="add_peer",
    )(x1, yt)


def _final(x1, yt, g, tm):
    t, d = x1.shape
    return pl.pallas_call(
        _final_kernel,
        grid=(t // tm,),
        in_specs=[pl.BlockSpec((tm, d), lambda i: (i, 0)), pl.BlockSpec((d, tm), lambda i: (0, i)),
                  pl.BlockSpec((1, d), lambda i: (0, 0))],
        out_specs=pl.BlockSpec((tm, d), lambda i: (i, 0)),
        out_shape=jax.ShapeDtypeStruct((t, d), F32),
        compiler_params=_cparams(("parallel",)),
        name="final",
    )(x1, yt, g)


def _pick(total, want):
    tile = min(total, want)
    assert total % tile == 0, (total, tile)
    return tile


def kernel(x, norm_mix, w_in, w_pool, pool_scale, w_out, norm_ffn, w_query, sub_keys,
           expert_u, expert_v, norm_final):
    b, s, d = x.shape
    t = b * s
    depth = norm_mix.shape[0]
    d_pool = w_pool.shape[1] * w_pool.shape[2]
    d_qkv = w_in.shape[2] - d_pool
    d_attn = d_qkv // 3
    hp = d_attn // LANES
    assert d_attn == w_out.shape[1] - d_pool and d_attn % LANES == 0
    assert sub_keys.shape[1:] == (PEER_HEADS, 2, N_KEYS, N_KEYS)
    assert expert_u.shape[1] == N_KEYS * N_KEYS
    assert s % (DILATED_PATTERNS[-1][1] * WBLK) == 0

    tm = _pick(s, 512)
    tm_route = _pick(s, 128)
    tm_peer = _pick(s, 512)
    ec = 512

    x2 = x.reshape(t, d)
    for layer in range(depth):
        qkv, zp = _in_proj(x2, norm_mix[layer][None], w_in[layer].astype(BF16), d_qkv, tm)
        o_list, l_list = [], []
        for window, dil in DILATED_PATTERNS:
            l = s // dil
            q6 = qkv.reshape(b, l, dil, 3, hp, LANES).transpose(3, 0, 2, 4, 1, 5)
            o, lse = _attention_branch(q6, window // dil)
            back = lambda a: a.transpose(0, 3, 1, 2, 4).reshape(t, d_attn)
            o_list.append(back(o))
            l_list.append(back(lse))
        x1, hf, hft = _mix_out(o_list, l_list, zp, x2, w_pool[layer].astype(BF16),
                               pool_scale[layer][None], w_out[layer].astype(BF16),
                               norm_ffn[layer][None], s, tm)
        rank1, e1, nk, e0 = _route(hf, w_query[layer].astype(BF16), sub_keys[layer].astype(BF16),
                                   tm_route)
        yt = _peer(expert_u[layer].astype(BF16), hft, expert_v[layer].astype(BF16).T,
                   rank1, e1, nk, e0, tm_peer, ec)
        if layer + 1 < depth:
            x2 = _add_peer(x1, yt, tm)
    out = _final(x1, yt, norm_final[None], tm)
    return out.reshape(b, s, d)
```

```python
import functools

import jax
import jax.numpy as jnp
from jax import lax
from jax.experimental import pallas as pl
from jax.experimental.pallas import tpu as pltpu

F32 = jnp.float32
BF16 = jnp.bfloat16

EPS = 1e-6
NEG = -1e30
HEAD_DIM = 64
LANES = 128
DILATED_PATTERNS = ((128, 1), (512, 4), (2048, 16))
WBLK = 128
POOL_WINDOWS = (2, 4, 8, 16)
POOL_HALO = 16
PEER_HEADS = 8
N_KEYS = 128
PEER_TOPK = 16
VMEM_LIMIT = 56 * 1024 * 1024


def _cparams(sem):
    return pltpu.CompilerParams(dimension_semantics=sem, vmem_limit_bytes=VMEM_LIMIT)


def _rms(x, g):
    ms = jnp.mean(x * x, axis=-1, keepdims=True)
    return x * lax.rsqrt(ms + EPS) * g


def _inproj_kernel(x_ref, g_ref, w_ref, qkv_ref, zp_ref, *, d_qkv):
    h = _rms(x_ref[...], g_ref[...]).astype(BF16)
    z = jnp.dot(h, w_ref[...], preferred_element_type=F32)
    qkv_ref[...] = z[:, :d_qkv].astype(BF16)
    zp_ref[...] = z[:, d_qkv:]


def _in_proj(x2, g, w_bf, d_qkv, tm):
    t, d = x2.shape
    e = w_bf.shape[1]
    return pl.pallas_call(
        functools.partial(_inproj_kernel, d_qkv=d_qkv),
        grid=(t // tm,),
        in_specs=[pl.BlockSpec((tm, d), lambda i: (i, 0)),
                  pl.BlockSpec((1, d), lambda i: (0, 0)),
                  pl.BlockSpec((d, e), lambda i: (0, 0))],
        out_specs=[pl.BlockSpec((tm, d_qkv), lambda i: (i, 0)),
                   pl.BlockSpec((tm, e - d_qkv), lambda i: (i, 0))],
        out_shape=[jax.ShapeDtypeStruct((t, d_qkv), BF16),
                   jax.ShapeDtypeStruct((t, e - d_qkv), F32)],
        compiler_params=_cparams(("parallel",)),
        name="in_proj",
    )(x2, g, w_bf)


def _attn_kernel(q_ref, kp_ref, kc_ref, vp_ref, vc_ref, o_ref, lse_ref, *, n_steps):
    n = pl.program_id(3)
    q = q_ref[...]
    kk = jnp.concatenate([kp_ref[...], kc_ref[...]], axis=0)
    vv = jnp.concatenate([vp_ref[...], vc_ref[...]], axis=0)
    lane_q = lax.broadcasted_iota(jnp.int32, q.shape, 1) < HEAD_DIM
    lane_v = lax.broadcasted_iota(jnp.int32, vv.shape, 1) < HEAD_DIM
    qi = lax.broadcasted_iota(jnp.int32, (WBLK, 2 * WBLK), 0)
    kj = lax.broadcasted_iota(jnp.int32, (WBLK, 2 * WBLK), 1)
    dist = qi + WBLK - kj
    mask = (dist >= 0) & (dist <= n_steps) & ((kj >= WBLK) | (n > 0))
    one = jnp.ones((), vv.dtype)
    zero = jnp.zeros((), q.dtype)

    def head(first):
        sel_q = lane_q if first else jnp.logical_not(lane_q)
        sel_v = lane_v if first else jnp.logical_not(lane_v)
        qh = jnp.where(sel_q, q, zero)
        s = lax.dot_general(qh, kk, (((1,), (1,)), ((), ())), preferred_element_type=F32)
        s = jnp.where(mask, s * (HEAD_DIM ** -0.5), NEG)
        m = jnp.max(s, axis=-1, keepdims=True)
        p = jnp.exp(s - m).astype(BF16)
        pv = jnp.dot(p, jnp.where(sel_v, vv, one), preferred_element_type=F32)
        den = pltpu.roll(pv, HEAD_DIM, 1)
        return pv / den, m + jnp.log(den)

    oa, la = head(True)
    ob, lb = head(False)
    out_lane = lax.broadcasted_iota(jnp.int32, oa.shape, 1) < HEAD_DIM
    o_ref[...] = jnp.where(out_lane, oa, ob)
    lse_ref[...] = jnp.where(out_lane, la, lb)


def _attention_branch(qkv6, n_steps):
    _, b, r, hp, l, _ = qkv6.shape
    nb = l // WBLK
    blk = (None, None, None, None, WBLK, LANES)

    def spec(which, prev):
        if prev:
            return pl.BlockSpec(blk, lambda bi, ri, hi, ni: (which, bi, ri, hi, jnp.maximum(ni - 1, 0), 0))
        return pl.BlockSpec(blk, lambda bi, ri, hi, ni: (which, bi, ri, hi, ni, 0))

    oblk = pl.BlockSpec((None, None, None, WBLK, LANES), lambda bi, ri, hi, ni: (bi, ri, hi, ni, 0))
    shape = jax.ShapeDtypeStruct((b, r, hp, l, LANES), F32)
    return pl.pallas_call(
        functools.partial(_attn_kernel, n_steps=n_steps),
        grid=(b, r, hp, nb),
        in_specs=[spec(0, False), spec(1, True), spec(1, False), spec(2, True), spec(2, False)],
        out_specs=[oblk, oblk],
        out_shape=[shape, shape],
        compiler_params=_cparams(("parallel", "parallel", "parallel", "arbitrary")),
        name=f"attn_r{r}",
    )(qkv6, qkv6, qkv6, qkv6, qkv6)


def _mixout_kernel(o1_ref, o2_ref, o3_ref, l1_ref, l2_ref, l3_ref, zp_ref, zh_ref, x_ref,
                   wp_ref, ps_ref, wo_ref, g_ref, x1_ref, hf_ref, hft_ref, *, tiles_per_seq):
    i = pl.program_id(0)
    tm = x_ref.shape[0]
    first = (i % tiles_per_seq) == 0

    l1, l2, l3 = l1_ref[...], l2_ref[...], l3_ref[...]
    lm = jnp.maximum(jnp.maximum(l1, l2), l3)
    e1, e2, e3 = jnp.exp(l1 - lm), jnp.exp(l2 - lm), jnp.exp(l3 - lm)
    att = (e1 * o1_ref[...] + e2 * o2_ref[...] + e3 * o3_ref[...]) / (e1 + e2 + e3)

    zc = zp_ref[...]
    halo = jnp.where(first, 0.0, zh_ref[...])
    buf = jnp.concatenate([zc, halo], axis=0)
    sums = {1: buf}
    w = 1
    while w < POOL_WINDOWS[-1]:
        sums[2 * w] = sums[w] + pltpu.roll(sums[w], w, 0)
        w *= 2
    pos = (i % tiles_per_seq) * tm + lax.broadcasted_iota(jnp.int32, (tm, 1), 0) + 1
    cg = zc.shape[1] // len(POOL_WINDOWS)
    mixed = []
    for g, win in enumerate(POOL_WINDOWS):
        cols = slice(g * cg, (g + 1) * cg)
        cnt = jnp.minimum(pos, win).astype(F32)
        pooled = sums[win][:tm, cols] / cnt - zc[:, cols]
        mixed.append(jnp.dot(pooled.astype(BF16), wp_ref[g], preferred_element_type=F32))
    mixed = jnp.concatenate(mixed, axis=1) * ps_ref[...]

    mix = jnp.concatenate([att, mixed], axis=1).astype(BF16)
    x1 = x_ref[...] + jnp.dot(mix, wo_ref[...], preferred_element_type=F32)
    x1_ref[...] = x1
    hf = _rms(x1, g_ref[...])
    hf_ref[...] = hf.astype(BF16)
    hft_ref[...] = hf.T.astype(BF16)


def _mix_out(o_list, l_list, zp, x2, wp_bf, ps, wo_bf, g, seq, tm):
    t, d = x2.shape
    da = o_list[0].shape[1]
    dp = zp.shape[1]
    row = lambda i: (i, 0)
    const2 = lambda i: (0, 0)
    halo_rows = tm // POOL_HALO
    return pl.pallas_call(
        functools.partial(_mixout_kernel, tiles_per_seq=seq // tm),
        grid=(t // tm,),
        in_specs=[pl.BlockSpec((tm, da), row)] * 6 + [
            pl.BlockSpec((tm, dp), row),
            pl.BlockSpec((POOL_HALO, dp), lambda i: (jnp.maximum(i * halo_rows - 1, 0), 0)),
            pl.BlockSpec((tm, d), row),
            pl.BlockSpec(wp_bf.shape, lambda i: (0, 0, 0)),
            pl.BlockSpec((1, dp), const2),
            pl.BlockSpec(wo_bf.shape, const2),
            pl.BlockSpec((1, d), const2)],
        out_specs=[pl.BlockSpec((tm, d), row), pl.BlockSpec((tm, d), row),
                   pl.BlockSpec((d, tm), lambda i: (0, i))],
        out_shape=[jax.ShapeDtypeStruct((t, d), F32), jax.ShapeDtypeStruct((t, d), BF16),
                   jax.ShapeDtypeStruct((d, t), BF16)],
        compiler_params=_cparams(("parallel",)),
        name="mix_out",
    )(*o_list, *l_list, zp, zp, x2, wp_bf, ps, wo_bf, g)


def _young_cells():
    return [(a, b) for a in range(PEER_TOPK) for b in range(PEER_TOPK)
            if (a + 1) * (b + 1) <= PEER_TOPK]


def _route_kernel(hf_ref, wq_ref, sk_ref, rank1_ref, e1_ref, nk_ref, e0_ref,
                  s_scr, rank_scr, val_scr, n_scr, aux_scr):
    tmr = hf_ref.shape[0]
    q = jnp.dot(hf_ref[...], wq_ref[...], preferred_element_type=F32).astype(BF16)
    for hp in range(2 * PEER_HEADS):
        s_scr[hp] = lax.dot_general(sk_ref[hp], q[:, hp * N_KEYS:(hp + 1) * N_KEYS],
                                    (((1,), (1,)), ((), ())), preferred_element_type=F32)

    key_id = lax.broadcasted_iota(jnp.int32, (N_KEYS, tmr), 0).astype(F32)

    for hp in range(2 * PEER_HEADS):
        h, p = divmod(hp, 2)

        def extract(r, carry):
            sw, rank = carry
            m = jnp.max(sw, axis=0, keepdims=True)
            idx = jnp.min(jnp.where(sw == m, key_id, float(N_KEYS)), axis=0, keepdims=True)
            sel = key_id == idx
            val_scr[p, r, pl.ds(h, 1), :] = m
            return jnp.where(sel, -jnp.inf, sw), jnp.where(sel, r.astype(F32), rank)

        _, rank = lax.fori_loop(0, PEER_TOPK, extract,
                                (s_scr[hp], jnp.full((N_KEYS, tmr), float(PEER_TOPK), F32)))
        rank_scr[hp] = rank

    cells = _young_cells()
    v0 = [val_scr[0, a] for a in range(PEER_TOPK)]
    v1 = [val_scr[1, b] for b in range(PEER_TOPK)]
    csum = {c: v0[c[0]] + v1[c[1]] for c in cells}
    beaten = {c: jnp.full(csum[c].shape, float((c[0] + 1) * (c[1] + 1) - 1), F32) for c in cells}
    for ix, cx in enumerate(cells):
        for cy in cells[ix + 1:]:
            comparable = (cx[0] <= cy[0] and cx[1] <= cy[1]) or (cy[0] <= cx[0] and cy[1] <= cx[1])
            if comparable:
                continue
            y_wins = jnp.where(csum[cy] > csum[cx], 1.0, 0.0)
            beaten[cx] = beaten[cx] + y_wins
            beaten[cy] = beaten[cy] + (1.0 - y_wins)
    top = csum[(0, 0)]
    zsum = jnp.zeros_like(top)
    ncol = [jnp.zeros_like(top) for _ in range(PEER_TOPK)]
    for c in cells:
        chosen = beaten[c] < float(PEER_TOPK)
        zsum = zsum + jnp.where(chosen, jnp.exp(csum[c] - top), 0.0)
        ncol[c[0]] = ncol[c[0]] + jnp.where(chosen, 1.0, 0.0)
    for a in range(PEER_TOPK):
        n_scr[a] = ncol[a]
    aux_scr[0] = 1.0 / zsum
    aux_scr[1] = v0[0]
    aux_scr[2] = v1[0]

    for h in range(PEER_HEADS):
        rank0 = rank_scr[2 * h]
        nk = jnp.zeros((N_KEYS, tmr), F32)
        for a in range(PEER_TOPK):
            nk = jnp.where(rank0 == float(a), n_scr[a, pl.ds(h, 1), :], nk)
        nk_ref[h] = nk
        e0_ref[h] = jnp.exp(s_scr[2 * h] - aux_scr[1, pl.ds(h, 1), :]) * aux_scr[0, pl.ds(h, 1), :]
        e1_ref[h] = jnp.exp(s_scr[2 * h + 1] - aux_scr[2, pl.ds(h, 1), :]).astype(BF16)
        rank1_ref[h] = rank_scr[2 * h + 1].astype(BF16)


def _route(hf, wq_bf, sk_bf, tmr):
    t, d = hf.shape
    sk2 = sk_bf.reshape(2 * PEER_HEADS, N_KEYS, sk_bf.shape[-1])
    oshape = lambda dt: jax.ShapeDtypeStruct((PEER_HEADS, N_KEYS, t), dt)
    ospec = pl.BlockSpec((PEER_HEADS, N_KEYS, tmr), lambda i: (0, 0, i))
    return pl.pallas_call(
        _route_kernel,
        grid=(t // tmr,),
        in_specs=[pl.BlockSpec((tmr, d), lambda i: (i, 0)),
                  pl.BlockSpec(wq_bf.shape, lambda i: (0, 0)),
                  pl.BlockSpec(sk2.shape, lambda i: (0, 0, 0))],
        out_specs=[ospec] * 4,
        out_shape=[oshape(BF16), oshape(BF16), oshape(F32), oshape(F32)],
        scratch_shapes=[pltpu.VMEM((2 * PEER_HEADS, N_KEYS, tmr), F32),
                        pltpu.VMEM((2 * PEER_HEADS, N_KEYS, tmr), F32),
                        pltpu.VMEM((2, PEER_TOPK, PEER_HEADS, tmr), F32),
                        pltpu.VMEM((PEER_TOPK, PEER_HEADS, tmr), F32),
                        pltpu.VMEM((3, PEER_HEADS, tmr), F32)],
        compiler_params=_cparams(("parallel",)),
        name="route",
    )(hf, wq_bf, sk2)


PACK = 16


def _gate_rows(row_ref, h, r, tm):
    half = jnp.broadcast_to(row_ref[h, r:r + 1, :], (PACK // 2, tm))
    return jnp.concatenate([half, half], axis=0).astype(BF16)


def _peer_gate_chunk(row0, s_ref, w_ref, rank1_ref, e1_ref, nk_ref, e0_ref):
    ec, tm = s_ref.shape
    subs = N_KEYS // PACK
    for j in range(ec // N_KEYS):
        gate = [jnp.zeros((PACK, tm), BF16) for _ in range(subs)]
        for h in range(PEER_HEADS):
            n_rows = _gate_rows(nk_ref, h, row0 + j, tm)
            e0_rows = _gate_rows(e0_ref, h, row0 + j, tm)
            for k in range(subs):
                rows = slice(k * PACK, (k + 1) * PACK)
                chosen = rank1_ref[h, rows, :] < n_rows
                picked = jnp.where(chosen, e1_ref[h, rows, :], jnp.zeros((), BF16))
                gate[k] = gate[k] + picked * e0_rows
        for k in range(subs):
            rows = slice(j * N_KEYS + k * PACK, j * N_KEYS + (k + 1) * PACK)
            s = s_ref[rows, :]
            act = 0.5 * s * (1.0 + lax.erf(s * (0.5 ** 0.5)))
            w_ref[rows, :] = act.astype(BF16) * gate[k]


def _peer_kernel(u_ref, hft_ref, vt_ref, rank1_ref, e1_ref, nk_ref, e0_ref, yt_ref, s_scr, w_scr):
    c = pl.program_id(1)

    @pl.when(c == 0)
    def _():
        yt_ref[...] = jnp.zeros_like(yt_ref)

    s_scr[...] = jnp.dot(u_ref[...], hft_ref[...], preferred_element_type=F32)
    _peer_gate_chunk(0, s_scr, w_scr, rank1_ref, e1_ref, nk_ref, e0_ref)
    yt_ref[...] += jnp.dot(vt_ref[...], w_scr[...], preferred_element_type=F32)


def _peer(u_bf, hft, vt_bf, rank1, e1, nk, e0, tm, ec):
    n_exp, d = u_bf.shape
    t = hft.shape[1]
    rspec = pl.BlockSpec((PEER_HEADS, N_KEYS, tm), lambda i, c: (0, 0, i))
    rows = pl.BlockSpec((PEER_HEADS, ec // N_KEYS, tm), lambda i, c: (0, c, i))
    return pl.pallas_call(
        _peer_kernel,
        grid=(t // tm, n_exp // ec),
        in_specs=[pl.BlockSpec((ec, d), lambda i, c: (c, 0)),
                  pl.BlockSpec((d, tm), lambda i, c: (0, i)),
                  pl.BlockSpec((d, ec), lambda i, c: (0, c)),
                  rspec, rspec, rows, rows],
        out_specs=pl.BlockSpec((d, tm), lambda i, c: (0, i)),
        out_shape=jax.ShapeDtypeStruct((d, t), F32),
        scratch_shapes=[pltpu.VMEM((ec, tm), F32), pltpu.VMEM((ec, tm), BF16)],
        compiler_params=_cparams(("parallel", "arbitrary")),
        name="peer",
    )(u_bf, hft, vt_bf, rank1, e1, nk, e0)


def _final_kernel(x1_ref, yt_ref, g_ref, o_ref):
    o_ref[...] = _rms(x1_ref[...] + yt_ref[...].T, g_ref[...])


def _add_peer_kernel(x1_ref, yt_ref, o_ref):
    o_ref[...] = x1_ref[...] + yt_ref[...].T


def _add_peer(x1, yt, tm):
    t, d = x1.shape
    return pl.pallas_call(
        _add_peer_kernel,
        grid=(t // tm,),
        in_specs=[pl.BlockSpec((tm, d), lambda i: (i, 0)), pl.BlockSpec((d, tm), lambda i: (0, i))],
        out_specs=pl.BlockSpec((tm, d), lambda i: (i, 0)),
        out_shape=jax.ShapeDtypeStruct((t, d), F32),
        compiler_params=_cparams(("parallel",)),
        name="add_peer",
    )(x1, yt)


def _final(x1, yt, g, tm):
    t, d = x1.shape
    return pl.pallas_call(
        _final_kernel,
        grid=(t // tm,),
        in_specs=[pl.BlockSpec((tm, d), lambda i: (i, 0)), pl.BlockSpec((d, tm), lambda i: (0, i)),
                  pl.BlockSpec((1, d), lambda i: (0, 0))],
        out_specs=pl.BlockSpec((tm, d), lambda i: (i, 0)),
        out_shape=jax.ShapeDtypeStruct((t, d), F32),
        compiler_params=_cparams(("parallel",)),
        name="final",
    )(x1, yt, g)


def _pick(total, want):
    tile = min(total, want)
    assert total % tile == 0, (total, tile)
    return tile


def kernel(x, norm_mix, w_in, w_pool, pool_scale, w_out, norm_ffn, w_query, sub_keys,
           expert_u, expert_v, norm_final):
    b, s, d = x.shape
    t = b * s
    depth = norm_mix.shape[0]
    d_pool = w_pool.shape[1] * w_pool.shape[2]
    d_qkv = w_in.shape[2] - d_pool
    d_attn = d_qkv // 3
    hp = d_attn // LANES
    assert d_attn == w_out.shape[1] - d_pool and d_attn % LANES == 0
    assert sub_keys.shape[1:] == (PEER_HEADS, 2, N_KEYS, N_KEYS)
    assert expert_u.shape[1] == N_KEYS * N_KEYS
    assert s % (DILATED_PATTERNS[-1][1] * WBLK) == 0

    tm = _pick(s, 512)
    tm_route = _pick(s, 128)
    tm_peer = _pick(s, 512)
    ec = 1024

    x2 = x.reshape(t, d)
    for layer in range(depth):
        qkv, zp = _in_proj(x2, norm_mix[layer][None], w_in[layer].astype(BF16), d_qkv, tm)
        o_list, l_list = [], []
        for window, dil in DILATED_PATTERNS:
            l = s // dil
            q6 = qkv.reshape(b, l, dil, 3, hp, LANES).transpose(3, 0, 2, 4, 1, 5)
            o, lse = _attention_branch(q6, window // dil)
            back = lambda a: a.transpose(0, 3, 1, 2, 4).reshape(t, d_attn)
            o_list.append(back(o))
            l_list.append(back(lse))
        x1, hf, hft = _mix_out(o_list, l_list, zp, x2, w_pool[layer].astype(BF16),
                               pool_scale[layer][None], w_out[layer].astype(BF16),
                               norm_ffn[layer][None], s, tm)
        rank1, e1, nk, e0 = _route(hf, w_query[layer].astype(BF16), sub_keys[layer].astype(BF16),
                                   tm_route)
        yt = _peer(expert_u[layer].astype(BF16), hft, expert_v[layer].astype(BF16).T,
                   rank1, e1, nk, e0, tm_peer, ec)
        if layer + 1 < depth:
            x2 = _add_peer(x1, yt, tm)
    out = _final(x1, yt, norm_final[None], tm)
    return out.reshape(b, s, d)
```

```python
import functools

import jax
import jax.numpy as jnp
from jax import lax
from jax.experimental import pallas as pl
from jax.experimental.pallas import tpu as pltpu

F32 = jnp.float32
BF16 = jnp.bfloat16

EPS = 1e-6
NEG = -1e30
HEAD_DIM = 64
LANES = 128
DILATED_PATTERNS = ((128, 1), (512, 4), (2048, 16))
WBLK = 128
POOL_WINDOWS = (2, 4, 8, 16)
POOL_HALO = 16
PEER_HEADS = 8
N_KEYS = 128
PEER_TOPK = 16
VMEM_LIMIT = 56 * 1024 * 1024


def _cparams(sem):
    return pltpu.CompilerParams(dimension_semantics=sem, vmem_limit_bytes=VMEM_LIMIT)


def _rms(x, g):
    ms = jnp.mean(x * x, axis=-1, keepdims=True)
    return x * lax.rsqrt(ms + EPS) * g


def _inproj_kernel(x_ref, g_ref, w_ref, qkv_ref, zp_ref, *, d_qkv):
    h = _rms(x_ref[...], g_ref[...]).astype(BF16)
    z = jnp.dot(h, w_ref[...], preferred_element_type=F32)
    qkv_ref[...] = z[:, :d_qkv].astype(BF16)
    zp_ref[...] = z[:, d_qkv:]


def _in_proj(x2, g, w_bf, d_qkv, tm):
    t, d = x2.shape
    e = w_bf.shape[1]
    return pl.pallas_call(
        functools.partial(_inproj_kernel, d_qkv=d_qkv),
        grid=(t // tm,),
        in_specs=[pl.BlockSpec((tm, d), lambda i: (i, 0)),
                  pl.BlockSpec((1, d), lambda i: (0, 0)),
                  pl.BlockSpec((d, e), lambda i: (0, 0))],
        out_specs=[pl.BlockSpec((tm, d_qkv), lambda i: (i, 0)),
                   pl.BlockSpec((tm, e - d_qkv), lambda i: (i, 0))],
        out_shape=[jax.ShapeDtypeStruct((t, d_qkv), BF16),
                   jax.ShapeDtypeStruct((t, e - d_qkv), F32)],
        compiler_params=_cparams(("parallel",)),
        name="in_proj",
    )(x2, g, w_bf)


def _attn_kernel(q_ref, kp_ref, kc_ref, vp_ref, vc_ref, o_ref, lse_ref, *, n_steps):
    n = pl.program_id(3)
    q = q_ref[...]
    kk = jnp.concatenate([kp_ref[...], kc_ref[...]], axis=0)
    vv = jnp.concatenate([vp_ref[...], vc_ref[...]], axis=0)
    lane_q = lax.broadcasted_iota(jnp.int32, q.shape, 1) < HEAD_DIM
    lane_v = lax.broadcasted_iota(jnp.int32, vv.shape, 1) < HEAD_DIM
    qi = lax.broadcasted_iota(jnp.int32, (WBLK, 2 * WBLK), 0)
    kj = lax.broadcasted_iota(jnp.int32, (WBLK, 2 * WBLK), 1)
    dist = qi + WBLK - kj
    mask = (dist >= 0) & (dist <= n_steps) & ((kj >= WBLK) | (n > 0))
    one = jnp.ones((), vv.dtype)
    zero = jnp.zeros((), q.dtype)

    def head(first):
        sel_q = lane_q if first else jnp.logical_not(lane_q)
        sel_v = lane_v if first else jnp.logical_not(lane_v)
        qh = jnp.where(sel_q, q, zero)
        s = lax.dot_general(qh, kk, (((1,), (1,)), ((), ())), preferred_element_type=F32)
        s = jnp.where(mask, s * (HEAD_DIM ** -0.5), NEG)
        m = jnp.max(s, axis=-1, keepdims=True)
        p = jnp.exp(s - m).astype(BF16)
        pv = jnp.dot(p, jnp.where(sel_v, vv, one), preferred_element_type=F32)
        den = pltpu.roll(pv, HEAD_DIM, 1)
        return pv / den, m + jnp.log(den)

    oa, la = head(True)
    ob, lb = head(False)
    out_lane = lax.broadcasted_iota(jnp.int32, oa.shape, 1) < HEAD_DIM
    o_ref[...] = jnp.where(out_lane, oa, ob)
    lse_ref[...] = jnp.where(out_lane, la, lb)


def _attention_branch(qkv6, n_steps):
    _, b, r, hp, l, _ = qkv6.shape
    nb = l // WBLK
    blk = (None, None, None, None, WBLK, LANES)

    def spec(which, prev):
        if prev:
            return pl.BlockSpec(blk, lambda bi, ri, hi, ni: (which, bi, ri, hi, jnp.maximum(ni - 1, 0), 0))
        return pl.BlockSpec(blk, lambda bi, ri, hi, ni: (which, bi, ri, hi, ni, 0))

    oblk = pl.BlockSpec((None, None, None, WBLK, LANES), lambda bi, ri, hi, ni: (bi, ri, hi, ni, 0))
    shape = jax.ShapeDtypeStruct((b, r, hp, l, LANES), F32)
    return pl.pallas_call(
        functools.partial(_attn_kernel, n_steps=n_steps),
        grid=(b, r, hp, nb),
        in_specs=[spec(0, False), spec(1, True), spec(1, False), spec(2, True), spec(2, False)],
        out_specs=[oblk, oblk],
        out_shape=[shape, shape],
        compiler_params=_cparams(("parallel", "parallel", "parallel", "arbitrary")),
        name=f"attn_r{r}",
    )(qkv6, qkv6, qkv6, qkv6, qkv6)


def _mixout_kernel(o1_ref, o2_ref, o3_ref, l1_ref, l2_ref, l3_ref, zp_ref, zh_ref, x_ref,
                   wp_ref, ps_ref, wo_ref, g_ref, x1_ref, hf_ref, hft_ref, *, tiles_per_seq):
    i = pl.program_id(0)
    tm = x_ref.shape[0]
    first = (i % tiles_per_seq) == 0

    l1, l2, l3 = l1_ref[...], l2_ref[...], l3_ref[...]
    lm = jnp.maximum(jnp.maximum(l1, l2), l3)
    e1, e2, e3 = jnp.exp(l1 - lm), jnp.exp(l2 - lm), jnp.exp(l3 - lm)
    att = (e1 * o1_ref[...] + e2 * o2_ref[...] + e3 * o3_ref[...]) / (e1 + e2 + e3)

    zc = zp_ref[...]
    halo = jnp.where(first, 0.0, zh_ref[...])
    buf = jnp.concatenate([zc, halo], axis=0)
    sums = {1: buf}
    w = 1
    while w < POOL_WINDOWS[-1]:
        sums[2 * w] = sums[w] + pltpu.roll(sums[w], w, 0)
        w *= 2
    pos = (i % tiles_per_seq) * tm + lax.broadcasted_iota(jnp.int32, (tm, 1), 0) + 1
    cg = zc.shape[1] // len(POOL_WINDOWS)
    mixed = []
    for g, win in enumerate(POOL_WINDOWS):
        cols = slice(g * cg, (g + 1) * cg)
        cnt = jnp.minimum(pos, win).astype(F32)
        pooled = sums[win][:tm, cols] / cnt - zc[:, cols]
        mixed.append(jnp.dot(pooled.astype(BF16), wp_ref[g], preferred_element_type=F32))
    mixed = jnp.concatenate(mixed, axis=1) * ps_ref[...]

    mix = jnp.concatenate([att, mixed], axis=1).astype(BF16)
    x1 = x_ref[...] + jnp.dot(mix, wo_ref[...], preferred_element_type=F32)
    x1_ref[...] = x1
    hf = _rms(x1, g_ref[...])
    hf_ref[...] = hf.astype(BF16)
    hft_ref[...] = hf.T.astype(BF16)


def _mix_out(o_list, l_list, zp, x2, wp_bf, ps, wo_bf, g, seq, tm):
    t, d = x2.shape
    da = o_list[0].shape[1]
    dp = zp.shape[1]
    row = lambda i: (i, 0)
    const2 = lambda i: (0, 0)
    halo_rows = tm // POOL_HALO
    return pl.pallas_call(
        functools.partial(_mixout_kernel, tiles_per_seq=seq // tm),
        grid=(t // tm,),
        in_specs=[pl.BlockSpec((tm, da), row)] * 6 + [
            pl.BlockSpec((tm, dp), row),
            pl.BlockSpec((POOL_HALO, dp), lambda i: (jnp.maximum(i * halo_rows - 1, 0), 0)),
            pl.BlockSpec((tm, d), row),
            pl.BlockSpec(wp_bf.shape, lambda i: (0, 0, 0)),
            pl.BlockSpec((1, dp), const2),
            pl.BlockSpec(wo_bf.shape, const2),
            pl.BlockSpec((1, d), const2)],
        out_specs=[pl.BlockSpec((tm, d), row), pl.BlockSpec((tm, d), row),
                   pl.BlockSpec((d, tm), lambda i: (0, i))],
        out_shape=[jax.ShapeDtypeStruct((t, d), F32), jax.ShapeDtypeStruct((t, d), BF16),
                   jax.ShapeDtypeStruct((d, t), BF16)],
        compiler_params=_cparams(("parallel",)),
        name="mix_out",
    )(*o_list, *l_list, zp, zp, x2, wp_bf, ps, wo_bf, g)


def _young_cells():
    return [(a, b) for a in range(PEER_TOPK) for b in range(PEER_TOPK)
            if (a + 1) * (b + 1) <= PEER_TOPK]


def _route_kernel(hf_ref, wq_ref, sk_ref, rank1_ref, e1_ref, nk_ref, e0_ref,
                  s_scr, rank_scr, val_scr, n_scr, aux_scr):
    tmr = hf_ref.shape[0]
    q = jnp.dot(hf_ref[...], wq_ref[...], preferred_element_type=F32).astype(BF16)
    for hp in range(2 * PEER_HEADS):
        s_scr[hp] = lax.dot_general(sk_ref[hp], q[:, hp * N_KEYS:(hp + 1) * N_KEYS],
                                    (((1,), (1,)), ((), ())), preferred_element_type=F32)

    no_rank = jnp.full((N_KEYS, tmr), float(PEER_TOPK), F32)
    miscount = jnp.zeros((1, tmr), F32)
    for hp in range(2 * PEER_HEADS):
        h, p = divmod(hp, 2)
        sw, rank = s_scr[hp], no_rank
        for r in range(PEER_TOPK):
            m = jnp.max(sw, axis=0, keepdims=True)
            hit = sw == m
            val_scr[p, r, h:h + 1, :] = m
            sw = jnp.where(hit, -jnp.inf, sw)
            rank = jnp.where(hit, float(r), rank)
        rank_scr[hp] = rank
        ranked = jnp.sum(jnp.where(rank < float(PEER_TOPK), 1.0, 0.0), axis=0, keepdims=True)
        miscount = jnp.maximum(miscount, jnp.abs(ranked - float(PEER_TOPK)))

    @pl.when(jnp.max(miscount) > 0.0)
    def _():
        key_id = lax.broadcasted_iota(jnp.int32, (N_KEYS, tmr), 0).astype(F32)
        for hp in range(2 * PEER_HEADS):
            h, p = divmod(hp, 2)

            def extract(r, carry):
                sw, rank = carry
                m = jnp.max(sw, axis=0, keepdims=True)
                idx = jnp.min(jnp.where(sw == m, key_id, float(N_KEYS)), axis=0, keepdims=True)
                sel = key_id == idx
                val_scr[p, r, pl.ds(h, 1), :] = m
                return jnp.where(sel, -jnp.inf, sw), jnp.where(sel, lax.convert_element_type(r, F32), rank)

            _, rank = lax.fori_loop(0, PEER_TOPK, extract, (s_scr[hp], no_rank))
            rank_scr[hp] = rank

    cells = _young_cells()
    v0 = [val_scr[0, a] for a in range(PEER_TOPK)]
    v1 = [val_scr[1, b] for b in range(PEER_TOPK)]
    csum = {c: v0[c[0]] + v1[c[1]] for c in cells}
    beaten = {c: jnp.full(csum[c].shape, float((c[0] + 1) * (c[1] + 1) - 1), F32) for c in cells}
    for ix, cx in enumerate(cells):
        for cy in cells[ix + 1:]:
            comparable = (cx[0] <= cy[0] and cx[1] <= cy[1]) or (cy[0] <= cx[0] and cy[1] <= cx[1])
            if comparable:
                continue
            y_wins = jnp.where(csum[cy] > csum[cx], 1.0, 0.0)
            beaten[cx] = beaten[cx] + y_wins
            beaten[cy] = beaten[cy] + (1.0 - y_wins)
    top = csum[(0, 0)]
    zsum = jnp.zeros_like(top)
    ncol = [jnp.zeros_like(top) for _ in range(PEER_TOPK)]
    for c in cells:
        chosen = beaten[c] < float(PEER_TOPK)
        zsum = zsum + jnp.where(chosen, jnp.exp(csum[c] - top), 0.0)
        ncol[c[0]] = ncol[c[0]] + jnp.where(chosen, 1.0, 0.0)
    for a in range(PEER_TOPK):
        n_scr[a] = ncol[a]
    aux_scr[0] = 1.0 / zsum
    aux_scr[1] = v0[0]
    aux_scr[2] = v1[0]

    for h in range(PEER_HEADS):
        rank0 = rank_scr[2 * h]
        nk = jnp.zeros((N_KEYS, tmr), F32)
        for a in range(PEER_TOPK):
            nk = jnp.where(rank0 == float(a), n_scr[a, pl.ds(h, 1), :], nk)
        nk_ref[h] = nk
        e0_ref[h] = jnp.exp(s_scr[2 * h] - aux_scr[1, pl.ds(h, 1), :]) * aux_scr[0, pl.ds(h, 1), :]
        e1_ref[h] = jnp.exp(s_scr[2 * h + 1] - aux_scr[2, pl.ds(h, 1), :]).astype(BF16)
        rank1_ref[h] = rank_scr[2 * h + 1].astype(BF16)


def _route(hf, wq_bf, sk_bf, tmr):
    t, d = hf.shape
    sk2 = sk_bf.reshape(2 * PEER_HEADS, N_KEYS, sk_bf.shape[-1])
    oshape = lambda dt: jax.ShapeDtypeStruct((PEER_HEADS, N_KEYS, t), dt)
    ospec = pl.BlockSpec((PEER_HEADS, N_KEYS, tmr), lambda i: (0, 0, i))
    return pl.pallas_call(
        _route_kernel,
        grid=(t // tmr,),
        in_specs=[pl.BlockSpec((tmr, d), lambda i: (i, 0)),
                  pl.BlockSpec(wq_bf.shape, lambda i: (0, 0)),
                  pl.BlockSpec(sk2.shape, lambda i: (0, 0, 0))],
        out_specs=[ospec] * 4,
        out_shape=[oshape(BF16), oshape(BF16), oshape(F32), oshape(F32)],
        scratch_shapes=[pltpu.VMEM((2 * PEER_HEADS, N_KEYS, tmr), F32),
                        pltpu.VMEM((2 * PEER_HEADS, N_KEYS, tmr), F32),
                        pltpu.VMEM((2, PEER_TOPK, PEER_HEADS, tmr), F32),
                        pltpu.VMEM((PEER_TOPK, PEER_HEADS, tmr), F32),
                        pltpu.VMEM((3, PEER_HEADS, tmr), F32)],
        compiler_params=_cparams(("parallel",)),
        name="route",
    )(hf, wq_bf, sk2)


PACK = 16


def _gate_rows(row_ref, h, r, tm):
    half = jnp.broadcast_to(row_ref[h, r:r + 1, :], (PACK // 2, tm))
    return jnp.concatenate([half, half], axis=0).astype(BF16)


def _peer_gate_chunk(row0, s_ref, w_ref, rank1_ref, e1_ref, nk_ref, e0_ref):
    ec, tm = s_ref.shape
    subs = N_KEYS // PACK
    for j in range(ec // N_KEYS):
        gate = [jnp.zeros((PACK, tm), BF16) for _ in range(subs)]
        for h in range(PEER_HEADS):
            n_rows = _gate_rows(nk_ref, h, row0 + j, tm)
            e0_rows = _gate_rows(e0_ref, h, row0 + j, tm)
            for k in range(subs):
                rows = slice(k * PACK, (k + 1) * PACK)
                chosen = rank1_ref[h, rows, :] < n_rows
                picked = jnp.where(chosen, e1_ref[h, rows, :], jnp.zeros((), BF16))
                gate[k] = gate[k] + picked * e0_rows
        for k in range(subs):
            rows = slice(j * N_KEYS + k * PACK, j * N_KEYS + (k + 1) * PACK)
            s = s_ref[rows, :]
            act = 0.5 * s * (1.0 + lax.erf(s * (0.5 ** 0.5)))
            w_ref[rows, :] = act.astype(BF16) * gate[k]


def _peer_kernel(u_ref, hft_ref, vt_ref, rank1_ref, e1_ref, nk_ref, e0_ref, yt_ref, s_scr, w_scr):
    c = pl.program_id(1)

    @pl.when(c == 0)
    def _():
        yt_ref[...] = jnp.zeros_like(yt_ref)

    s_scr[...] = jnp.dot(u_ref[...], hft_ref[...], preferred_element_type=F32)
    _peer_gate_chunk(0, s_scr, w_scr, rank1_ref, e1_ref, nk_ref, e0_ref)
    yt_ref[...] += jnp.dot(vt_ref[...], w_scr[...], preferred_element_type=F32)


def _peer(u_bf, hft, vt_bf, rank1, e1, nk, e0, tm, ec):
    n_exp, d = u_bf.shape
    t = hft.shape[1]
    rspec = pl.BlockSpec((PEER_HEADS, N_KEYS, tm), lambda i, c: (0, 0, i))
    rows = pl.BlockSpec((PEER_HEADS, ec // N_KEYS, tm), lambda i, c: (0, c, i))
    return pl.pallas_call(
        _peer_kernel,
        grid=(t // tm, n_exp // ec),
        in_specs=[pl.BlockSpec((ec, d), lambda i, c: (c, 0)),
                  pl.BlockSpec((d, tm), lambda i, c: (0, i)),
                  pl.BlockSpec((d, ec), lambda i, c: (0, c)),
                  rspec, rspec, rows, rows],
        out_specs=pl.BlockSpec((d, tm), lambda i, c: (0, i)),
        out_shape=jax.ShapeDtypeStruct((d, t), F32),
        scratch_shapes=[pltpu.VMEM((ec, tm), F32), pltpu.VMEM((ec, tm), BF16)],
        compiler_params=_cparams(("parallel", "arbitrary")),
        name="peer",
    )(u_bf, hft, vt_bf, rank1, e1, nk, e0)


def _final_kernel(x1_ref, yt_ref, g_ref, o_ref):
    o_ref[...] = _rms(x1_ref[...] + yt_ref[...].T, g_ref[...])


def _add_peer_kernel(x1_ref, yt_ref, o_ref):
    o_ref[...] = x1_ref[...] + yt_ref[...].T


def _add_peer(x1, yt, tm):
    t, d = x1.shape
    return pl.pallas_call(
        _add_peer_kernel,
        grid=(t // tm,),
        in_specs=[pl.BlockSpec((tm, d), lambda i: (i, 0)), pl.BlockSpec((d, tm), lambda i: (0, i))],
        out_specs=pl.BlockSpec((tm, d), lambda i: (i, 0)),
        out_shape=jax.ShapeDtypeStruct((t, d), F32),
        compiler_params=_cparams(("parallel",)),
        name="add_peer",
    )(x1, yt)


def _final(x1, yt, g, tm):
    t, d = x1.shape
    return pl.pallas_call(
        _final_kernel,
        grid=(t // tm,),
        in_specs=[pl.BlockSpec((tm, d), lambda i: (i, 0)), pl.BlockSpec((d, tm), lambda i: (0, i)),
                  pl.BlockSpec((1, d), lambda i: (0, 0))],
        out_specs=pl.BlockSpec((tm, d), lambda i: (i, 0)),
        out_shape=jax.ShapeDtypeStruct((t, d), F32),
        compiler_params=_cparams(("parallel",)),
        name="final",
    )(x1, yt, g)


def _pick(total, want):
    tile = min(total, want)
    assert total % tile == 0, (total, tile)
    return tile


def kernel(x, norm_mix, w_in, w_pool, pool_scale, w_out, norm_ffn, w_query, sub_keys,
           expert_u, expert_v, norm_final):
    b, s, d = x.shape
    t = b * s
    depth = norm_mix.shape[0]
    d_pool = w_pool.shape[1] * w_pool.shape[2]
    d_qkv = w_in.shape[2] - d_pool
    d_attn = d_qkv // 3
    hp = d_attn // LANES
    assert d_attn == w_out.shape[1] - d_pool and d_attn % LANES == 0
    assert sub_keys.shape[1:] == (PEER_HEADS, 2, N_KEYS, N_KEYS)
    assert expert_u.shape[1] == N_KEYS * N_KEYS
    assert s % (DILATED_PATTERNS[-1][1] * WBLK) == 0

    tm = _pick(s, 512)
    tm_route = _pick(s, 128)
    tm_peer = _pick(s, 512)
    ec = 1024

    x2 = x.reshape(t, d)
    for layer in range(depth):
        qkv, zp = _in_proj(x2, norm_mix[layer][None], w_in[layer].astype(BF16), d_qkv, tm)
        o_list, l_list = [], []
        for window, dil in DILATED_PATTERNS:
            l = s // dil
            q6 = qkv.reshape(b, l, dil, 3, hp, LANES).transpose(3, 0, 2, 4, 1, 5)
            o, lse = _attention_branch(q6, window // dil)
            back = lambda a: a.transpose(0, 3, 1, 2, 4).reshape(t, d_attn)
            o_list.append(back(o))
            l_list.append(back(lse))
        x1, hf, hft = _mix_out(o_list, l_list, zp, x2, w_pool[layer].astype(BF16),
                               pool_scale[layer][None], w_out[layer].astype(BF16),
                               norm_ffn[layer][None], s, tm)
        rank1, e1, nk, e0 = _route(hf, w_query[layer].astype(BF16), sub_keys[layer].astype(BF16),
                                   tm_route)
        yt = _peer(expert_u[layer].astype(BF16), hft, expert_v[layer].astype(BF16).T,
                   rank1, e1, nk, e0, tm_peer, ec)
        if layer + 1 < depth:
            x2 = _add_peer(x1, yt, tm)
    out = _final(x1, yt, norm_final[None], tm)
    return out.reshape(b, s, d)
```

```python
import functools

import jax
import jax.numpy as jnp
from jax import lax
from jax.experimental import pallas as pl
from jax.experimental.pallas import tpu as pltpu

F32 = jnp.float32
BF16 = jnp.bfloat16

EPS = 1e-6
NEG = -1e30
HEAD_DIM = 64
LANES = 128
DILATED_PATTERNS = ((128, 1), (512, 4), (2048, 16))
WBLK = 128
POOL_WINDOWS = (2, 4, 8, 16)
POOL_HALO = 16
PEER_HEADS = 8
N_KEYS = 128
PEER_TOPK = 16
VMEM_LIMIT = 56 * 1024 * 1024


def _cparams(sem):
    return pltpu.CompilerParams(dimension_semantics=sem, vmem_limit_bytes=VMEM_LIMIT)


def _rms(x, g):
    ms = jnp.mean(x * x, axis=-1, keepdims=True)
    return x * lax.rsqrt(ms + EPS) * g


def _inproj_kernel(x_ref, g_ref, w_ref, qkv_ref, zp_ref):
    h = _rms(x_ref[...], g_ref[...]).astype(BF16)
    z = jnp.dot(h, w_ref[...], preferred_element_type=F32)
    n_blocks = qkv_ref.shape[0]
    for j in range(n_blocks):
        qkv_ref[j] = z[:, j * LANES:(j + 1) * LANES]
    zp_ref[...] = z[:, n_blocks * LANES:]


def _in_proj(x2, g, w_bf, d_qkv, tm):
    t, d = x2.shape
    e = w_bf.shape[1]
    n_blocks = d_qkv // LANES
    return pl.pallas_call(
        _inproj_kernel,
        grid=(t // tm,),
        in_specs=[pl.BlockSpec((tm, d), lambda i: (i, 0)),
                  pl.BlockSpec((1, d), lambda i: (0, 0)),
                  pl.BlockSpec((d, e), lambda i: (0, 0))],
        out_specs=[pl.BlockSpec((n_blocks, tm, LANES), lambda i: (0, i, 0)),
                   pl.BlockSpec((tm, e - d_qkv), lambda i: (i, 0))],
        out_shape=[jax.ShapeDtypeStruct((n_blocks, t, LANES), F32),
                   jax.ShapeDtypeStruct((t, e - d_qkv), F32)],
        compiler_params=_cparams(("parallel",)),
        name="in_proj",
    )(x2, g, w_bf)


ATTN_TILE = DILATED_PATTERNS[-1][1] * WBLK


def _rows(ref, start, size, stride):
    if stride == 1:
        return ref[start:start + size, :]
    return ref[pl.ds(start, size, stride=stride), :]


def _attn_block(q, kk, vv, mask):
    lane_q = lax.broadcasted_iota(jnp.int32, q.shape, 1) < HEAD_DIM
    lane_v = lax.broadcasted_iota(jnp.int32, vv.shape, 1) < HEAD_DIM
    one = jnp.ones((), vv.dtype)
    zero = jnp.zeros((), q.dtype)

    def head(first):
        sel_q = lane_q if first else jnp.logical_not(lane_q)
        sel_v = lane_v if first else jnp.logical_not(lane_v)
        qh = jnp.where(sel_q, q, zero)
        s = lax.dot_general(qh, kk, (((1,), (1,)), ((), ())), preferred_element_type=F32)
        s = jnp.where(mask, s * (HEAD_DIM ** -0.5), NEG)
        m = jnp.max(s, axis=-1, keepdims=True)
        p = jnp.exp(s - m).astype(BF16)
        pv = jnp.dot(p, jnp.where(sel_v, vv, one), preferred_element_type=F32)
        den = pltpu.roll(pv, HEAD_DIM, 1)
        return pv / den, m + jnp.log(den)

    oa, la = head(True)
    ob, lb = head(False)
    return jnp.where(lane_q, oa, ob), jnp.where(lane_q, la, lb)


def _attn_kernel(q_ref, kp_ref, kc_ref, vp_ref, vc_ref, att_ref, o_scr, l_scr):
    n = pl.program_id(2)
    tile = q_ref.shape[0]
    qi = lax.broadcasted_iota(jnp.int32, (WBLK, 2 * WBLK), 0)
    kj = lax.broadcasted_iota(jnp.int32, (WBLK, 2 * WBLK), 1)
    dist = qi + WBLK - kj
    for bi, (window, dil) in enumerate(DILATED_PATTERNS):
        band = (dist >= 0) & (dist <= window // dil)
        band_first = band & ((kj >= WBLK) | (n > 0))
        sub = tile // dil
        for c in range(dil):
            q = _rows(q_ref, c, sub, dil).astype(BF16)
            halo = tile - WBLK * dil + c
            kcat = jnp.concatenate([_rows(kp_ref, halo, WBLK, dil), _rows(kc_ref, c, sub, dil)],
                                   axis=0).astype(BF16)
            vcat = jnp.concatenate([_rows(vp_ref, halo, WBLK, dil), _rows(vc_ref, c, sub, dil)],
                                   axis=0).astype(BF16)
            for j in range(sub // WBLK):
                o, lse = _attn_block(q[j * WBLK:(j + 1) * WBLK], kcat[j * WBLK:(j + 2) * WBLK],
                                     vcat[j * WBLK:(j + 2) * WBLK], band_first if j == 0 else band)
                first_row = c + dil * j * WBLK
                if dil == 1:
                    o_scr[bi, first_row:first_row + WBLK, :] = o
                    l_scr[bi, first_row:first_row + WBLK, :] = lse
                else:
                    o_scr[bi, pl.ds(first_row, WBLK, stride=dil), :] = o
                    l_scr[bi, pl.ds(first_row, WBLK, stride=dil), :] = lse

    l1, l2, l3 = l_scr[0], l_scr[1], l_scr[2]
    lm = jnp.maximum(jnp.maximum(l1, l2), l3)
    e1, e2, e3 = jnp.exp(l1 - lm), jnp.exp(l2 - lm), jnp.exp(l3 - lm)
    att_ref[...] = (e1 * o_scr[0] + e2 * o_scr[1] + e3 * o_scr[2]) / (e1 + e2 + e3)


def _attention(qkv, b, s):
    hp = qkv.shape[0] // 3
    tile = ATTN_TILE
    qkv5 = qkv.reshape(3 * hp, b, s, LANES)
    blk = (None, None, tile, LANES)

    def spec(which, prev):
        if prev:
            return pl.BlockSpec(blk, lambda bi, hi, ni: (which * hp + hi, bi, jnp.maximum(ni - 1, 0), 0))
        return pl.BlockSpec(blk, lambda bi, hi, ni: (which * hp + hi, bi, ni, 0))

    att = pl.pallas_call(
        _attn_kernel,
        grid=(b, hp, s // tile),
        in_specs=[spec(0, False), spec(1, True), spec(1, False), spec(2, True), spec(2, False)],
        out_specs=pl.BlockSpec(blk, lambda bi, hi, ni: (hi, bi, ni, 0)),
        out_shape=jax.ShapeDtypeStruct((hp, b, s, LANES), F32),
        scratch_shapes=[pltpu.VMEM((len(DILATED_PATTERNS), tile, LANES), F32),
                        pltpu.VMEM((len(DILATED_PATTERNS), tile, LANES), F32)],
        compiler_params=_cparams(("parallel", "parallel", "arbitrary")),
        name="attention",
    )(qkv5, qkv5, qkv5, qkv5, qkv5)
    return att.reshape(hp, b * s, LANES)


def _mixout_kernel(att_ref, zp_ref, zh_ref, x_ref, wp_ref, ps_ref, wo_ref, g_ref,
                   x1_ref, hf_ref, hft_ref, *, tiles_per_seq):
    i = pl.program_id(0)
    tm = x_ref.shape[0]
    first = (i % tiles_per_seq) == 0
    att = jnp.concatenate([att_ref[j] for j in range(att_ref.shape[0])], axis=1)

    zc = zp_ref[...]
    halo = jnp.where(first, 0.0, zh_ref[...])
    buf = jnp.concatenate([zc, halo], axis=0)
    sums = {1: buf}
    w = 1
    while w < POOL_WINDOWS[-1]:
        sums[2 * w] = sums[w] + pltpu.roll(sums[w], w, 0)
        w *= 2
    pos = (i % tiles_per_seq) * tm + lax.broadcasted_iota(jnp.int32, (tm, 1), 0) + 1
    cg = zc.shape[1] // len(POOL_WINDOWS)
    mixed = []
    for g, win in enumerate(POOL_WINDOWS):
        cols = slice(g * cg, (g + 1) * cg)
        cnt = jnp.minimum(pos, win).astype(F32)
        pooled = sums[win][:tm, cols] / cnt - zc[:, cols]
        mixed.append(jnp.dot(pooled.astype(BF16), wp_ref[g], preferred_element_type=F32))
    mixed = jnp.concatenate(mixed, axis=1) * ps_ref[...]

    mix = jnp.concatenate([att, mixed], axis=1).astype(BF16)
    x1 = x_ref[...] + jnp.dot(mix, wo_ref[...], preferred_element_type=F32)
    x1_ref[...] = x1
    hf = _rms(x1, g_ref[...])
    hf_ref[...] = hf.astype(BF16)
    hft_ref[...] = hf.T.astype(BF16)


def _mix_out(att, zp, x2, wp_bf, ps, wo_bf, g, seq, tm):
    t, d = x2.shape
    dp = zp.shape[1]
    row = lambda i: (i, 0)
    const2 = lambda i: (0, 0)
    halo_rows = tm // POOL_HALO
    return pl.pallas_call(
        functools.partial(_mixout_kernel, tiles_per_seq=seq // tm),
        grid=(t // tm,),
        in_specs=[
            pl.BlockSpec((att.shape[0], tm, LANES), lambda i: (0, i, 0)),
            pl.BlockSpec((tm, dp), row),
            pl.BlockSpec((POOL_HALO, dp), lambda i: (jnp.maximum(i * halo_rows - 1, 0), 0)),
            pl.BlockSpec((tm, d), row),
            pl.BlockSpec(wp_bf.shape, lambda i: (0, 0, 0)),
            pl.BlockSpec((1, dp), const2),
            pl.BlockSpec(wo_bf.shape, const2),
            pl.BlockSpec((1, d), const2)],
        out_specs=[pl.BlockSpec((tm, d), row), pl.BlockSpec((tm, d), row),
                   pl.BlockSpec((d, tm), lambda i: (0, i))],
        out_shape=[jax.ShapeDtypeStruct((t, d), F32), jax.ShapeDtypeStruct((t, d), BF16),
                   jax.ShapeDtypeStruct((d, t), BF16)],
        compiler_params=_cparams(("parallel",)),
        name="mix_out",
    )(att, zp, zp, x2, wp_bf, ps, wo_bf, g)


def _young_cells():
    return [(a, b) for a in range(PEER_TOPK) for b in range(PEER_TOPK)
            if (a + 1) * (b + 1) <= PEER_TOPK]


def _route_kernel(hf_ref, wq_ref, sk_ref, rank1_ref, e1_ref, nk_ref, e0_ref,
                  s_scr, rank_scr, val_scr, n_scr, aux_scr):
    tmr = hf_ref.shape[0]
    q = jnp.dot(hf_ref[...], wq_ref[...], preferred_element_type=F32).astype(BF16)
    for hp in range(2 * PEER_HEADS):
        s_scr[hp] = lax.dot_general(sk_ref[hp], q[:, hp * N_KEYS:(hp + 1) * N_KEYS],
                                    (((1,), (1,)), ((), ())), preferred_element_type=F32)

    no_rank = jnp.full((N_KEYS, tmr), float(PEER_TOPK), F32)
    miscount = jnp.zeros((1, tmr), F32)
    for hp in range(2 * PEER_HEADS):
        h, p = divmod(hp, 2)
        sw, rank = s_scr[hp], no_rank
        for r in range(PEER_TOPK):
            m = jnp.max(sw, axis=0, keepdims=True)
            hit = sw == m
            val_scr[p, r, h:h + 1, :] = m
            sw = jnp.where(hit, -jnp.inf, sw)
            rank = jnp.where(hit, float(r), rank)
        rank_scr[hp] = rank
        ranked = jnp.sum(jnp.where(rank < float(PEER_TOPK), 1.0, 0.0), axis=0, keepdims=True)
        miscount = jnp.maximum(miscount, jnp.abs(ranked - float(PEER_TOPK)))

    @pl.when(jnp.max(miscount) > 0.0)
    def _():
        key_id = lax.broadcasted_iota(jnp.int32, (N_KEYS, tmr), 0).astype(F32)
        for hp in range(2 * PEER_HEADS):
            h, p = divmod(hp, 2)

            def extract(r, carry):
                sw, rank = carry
                m = jnp.max(sw, axis=0, keepdims=True)
                idx = jnp.min(jnp.where(sw == m, key_id, float(N_KEYS)), axis=0, keepdims=True)
                sel = key_id == idx
                val_scr[p, r, pl.ds(h, 1), :] = m
                return jnp.where(sel, -jnp.inf, sw), jnp.where(sel, lax.convert_element_type(r, F32), rank)

            _, rank = lax.fori_loop(0, PEER_TOPK, extract, (s_scr[hp], no_rank))
            rank_scr[hp] = rank

    cells = _young_cells()
    v0 = [val_scr[0, a] for a in range(PEER_TOPK)]
    v1 = [val_scr[1, b] for b in range(PEER_TOPK)]
    csum = {c: v0[c[0]] + v1[c[1]] for c in cells}
    beaten = {c: jnp.full(csum[c].shape, float((c[0] + 1) * (c[1] + 1) - 1), F32) for c in cells}
    for ix, cx in enumerate(cells):
        for cy in cells[ix + 1:]:
            comparable = (cx[0] <= cy[0] and cx[1] <= cy[1]) or (cy[0] <= cx[0] and cy[1] <= cx[1])
            if comparable:
                continue
            y_wins = jnp.where(csum[cy] > csum[cx], 1.0, 0.0)
            beaten[cx] = beaten[cx] + y_wins
            beaten[cy] = beaten[cy] + (1.0 - y_wins)
    top = csum[(0, 0)]
    zsum = jnp.zeros_like(top)
    ncol = [jnp.zeros_like(top) for _ in range(PEER_TOPK)]
    for c in cells:
        chosen = beaten[c] < float(PEER_TOPK)
        zsum = zsum + jnp.where(chosen, jnp.exp(csum[c] - top), 0.0)
        ncol[c[0]] = ncol[c[0]] + jnp.where(chosen, 1.0, 0.0)
    for a in range(PEER_TOPK):
        n_scr[a] = ncol[a]
    aux_scr[0] = 1.0 / zsum
    aux_scr[1] = v0[0]
    aux_scr[2] = v1[0]

    for h in range(PEER_HEADS):
        rank0 = rank_scr[2 * h]
        nk = jnp.zeros((N_KEYS, tmr), F32)
        for a in range(PEER_TOPK):
            nk = jnp.where(rank0 == float(a), n_scr[a, pl.ds(h, 1), :], nk)
        nk_ref[h] = nk
        e0_ref[h] = jnp.exp(s_scr[2 * h] - aux_scr[1, pl.ds(h, 1), :]) * aux_scr[0, pl.ds(h, 1), :]
        e1_ref[h] = jnp.exp(s_scr[2 * h + 1] - aux_scr[2, pl.ds(h, 1), :]).astype(BF16)
        rank1_ref[h] = rank_scr[2 * h + 1].astype(BF16)


def _route(hf, wq_bf, sk_bf, tmr):
    t, d = hf.shape
    sk2 = sk_bf.reshape(2 * PEER_HEADS, N_KEYS, sk_bf.shape[-1])
    oshape = lambda dt: jax.ShapeDtypeStruct((PEER_HEADS, N_KEYS, t), dt)
    ospec = pl.BlockSpec((PEER_HEADS, N_KEYS, tmr), lambda i: (0, 0, i))
    return pl.pallas_call(
        _route_kernel,
        grid=(t // tmr,),
        in_specs=[pl.BlockSpec((tmr, d), lambda i: (i, 0)),
                  pl.BlockSpec(wq_bf.shape, lambda i: (0, 0)),
                  pl.BlockSpec(sk2.shape, lambda i: (0, 0, 0))],
        out_specs=[ospec] * 4,
        out_shape=[oshape(BF16), oshape(BF16), oshape(F32), oshape(F32)],
        scratch_shapes=[pltpu.VMEM((2 * PEER_HEADS, N_KEYS, tmr), F32),
                        pltpu.VMEM((2 * PEER_HEADS, N_KEYS, tmr), F32),
                        pltpu.VMEM((2, PEER_TOPK, PEER_HEADS, tmr), F32),
                        pltpu.VMEM((PEER_TOPK, PEER_HEADS, tmr), F32),
                        pltpu.VMEM((3, PEER_HEADS, tmr), F32)],
        compiler_params=_cparams(("parallel",)),
        name="route",
    )(hf, wq_bf, sk2)


PACK = 16


def _gate_rows(row_ref, h, r, tm):
    half = jnp.broadcast_to(row_ref[h, r:r + 1, :], (PACK // 2, tm))
    return jnp.concatenate([half, half], axis=0).astype(BF16)


def _peer_gate_chunk(row0, s_ref, w_ref, rank1_ref, e1_ref, nk_ref, e0_ref):
    ec, tm = s_ref.shape
    subs = N_KEYS // PACK
    for j in range(ec // N_KEYS):
        gate = [jnp.zeros((PACK, tm), BF16) for _ in range(subs)]
        for h in range(PEER_HEADS):
            n_rows = _gate_rows(nk_ref, h, row0 + j, tm)
            e0_rows = _gate_rows(e0_ref, h, row0 + j, tm)
            for k in range(subs):
                rows = slice(k * PACK, (k + 1) * PACK)
                chosen = rank1_ref[h, rows, :] < n_rows
                picked = jnp.where(chosen, e1_ref[h, rows, :], jnp.zeros((), BF16))
                gate[k] = gate[k] + picked * e0_rows
        for k in range(subs):
            rows = slice(j * N_KEYS + k * PACK, j * N_KEYS + (k + 1) * PACK)
            s = s_ref[rows, :]
            act = 0.5 * s * (1.0 + lax.erf(s * (0.5 ** 0.5)))
            w_ref[rows, :] = act.astype(BF16) * gate[k]


def _peer_kernel(u_ref, hft_ref, vt_ref, rank1_ref, e1_ref, nk_ref, e0_ref, yt_ref, s_scr, w_scr):
    c = pl.program_id(1)

    @pl.when(c == 0)
    def _():
        yt_ref[...] = jnp.zeros_like(yt_ref)

    s_scr[...] = jnp.dot(u_ref[...], hft_ref[...], preferred_element_type=F32)
    _peer_gate_chunk(0, s_scr, w_scr, rank1_ref, e1_ref, nk_ref, e0_ref)
    yt_ref[...] += jnp.dot(vt_ref[...], w_scr[...], preferred_element_type=F32)


def _peer(u_bf, hft, vt_bf, rank1, e1, nk, e0, tm, ec):
    n_exp, d = u_bf.shape
    t = hft.shape[1]
    rspec = pl.BlockSpec((PEER_HEADS, N_KEYS, tm), lambda i, c: (0, 0, i))
    rows = pl.BlockSpec((PEER_HEADS, ec // N_KEYS, tm), lambda i, c: (0, c, i))
    return pl.pallas_call(
        _peer_kernel,
        grid=(t // tm, n_exp // ec),
        in_specs=[pl.BlockSpec((ec, d), lambda i, c: (c, 0)),
                  pl.BlockSpec((d, tm), lambda i, c: (0, i)),
                  pl.BlockSpec((d, ec), lambda i, c: (0, c)),
                  rspec, rspec, rows, rows],
        out_specs=pl.BlockSpec((d, tm), lambda i, c: (0, i)),
        out_shape=jax.ShapeDtypeStruct((d, t), F32),
        scratch_shapes=[pltpu.VMEM((ec, tm), F32), pltpu.VMEM((ec, tm), BF16)],
        compiler_params=_cparams(("parallel", "arbitrary")),
        name="peer",
    )(u_bf, hft, vt_bf, rank1, e1, nk, e0)


def _final_kernel(x1_ref, yt_ref, g_ref, o_ref):
    o_ref[...] = _rms(x1_ref[...] + yt_ref[...].T, g_ref[...])


def _add_peer_kernel(x1_ref, yt_ref, o_ref):
    o_ref[...] = x1_ref[...] + yt_ref[...].T


def _add_peer(x1, yt, tm):
    t, d = x1.shape
    return pl.pallas_call(
        _add_peer_kernel,
        grid=(t // tm,),
        in_specs=[pl.BlockSpec((tm, d), lambda i: (i, 0)), pl.BlockSpec((d, tm), lambda i: (0, i))],
        out_specs=pl.BlockSpec((tm, d), lambda i: (i, 0)),
        out_shape=jax.ShapeDtypeStruct((t, d), F32),
        compiler_params=_cparams(("parallel",)),
        name="add_peer",
    )(x1, yt)


def _final(x1, yt, g, tm):
    t, d = x1.shape
    return pl.pallas_call(
        _final_kernel,
        grid=(t // tm,),
        in_specs=[pl.BlockSpec((tm, d), lambda i: (i, 0)), pl.BlockSpec((d, tm), lambda i: (0, i)),
                  pl.BlockSpec((1, d), lambda i: (0, 0))],
        out_specs=pl.BlockSpec((tm, d), lambda i: (i, 0)),
        out_shape=jax.ShapeDtypeStruct((t, d), F32),
        compiler_params=_cparams(("parallel",)),
        name="final",
    )(x1, yt, g)


def _pick(total, want):
    tile = min(total, want)
    assert total % tile == 0, (total, tile)
    return tile


def kernel(x, norm_mix, w_in, w_pool, pool_scale, w_out, norm_ffn, w_query, sub_keys,
           expert_u, expert_v, norm_final):
    b, s, d = x.shape
    t = b * s
    depth = norm_mix.shape[0]
    d_pool = w_pool.shape[1] * w_pool.shape[2]
    d_qkv = w_in.shape[2] - d_pool
    d_attn = d_qkv // 3
    hp = d_attn // LANES
    assert d_attn == w_out.shape[1] - d_pool and d_attn % LANES == 0
    assert sub_keys.shape[1:] == (PEER_HEADS, 2, N_KEYS, N_KEYS)
    assert expert_u.shape[1] == N_KEYS * N_KEYS
    assert s % (DILATED_PATTERNS[-1][1] * WBLK) == 0

    tm = _pick(s, 512)
    tm_route = _pick(s, 128)
    tm_peer = _pick(s, 512)
    ec = 1024

    x2 = x.reshape(t, d)
    for layer in range(depth):
        qkv, zp = _in_proj(x2, norm_mix[layer][None], w_in[layer].astype(BF16), d_qkv, tm)
        att = _attention(qkv, b, s)
        x1, hf, hft = _mix_out(att, zp, x2, w_pool[layer].astype(BF16),
                               pool_scale[layer][None], w_out[layer].astype(BF16),
                               norm_ffn[layer][None], s, tm)
        rank1, e1, nk, e0 = _route(hf, w_query[layer].astype(BF16), sub_keys[layer].astype(BF16),
                                   tm_route)
        yt = _peer(expert_u[layer].astype(BF16), hft, expert_v[layer].astype(BF16).T,
                   rank1, e1, nk, e0, tm_peer, ec)
        if layer + 1 < depth:
            x2 = _add_peer(x1, yt, tm)
    out = _final(x1, yt, norm_final[None], tm)
    return out.reshape(b, s, d)
```

```python
import functools

import jax
import jax.numpy as jnp
from jax import lax
from jax.experimental import pallas as pl
from jax.experimental.pallas import tpu as pltpu

F32 = jnp.float32
BF16 = jnp.bfloat16

EPS = 1e-6
NEG = -1e30
HEAD_DIM = 64
LANES = 128
DILATED_PATTERNS = ((128, 1), (512, 4), (2048, 16))
WBLK = 128
POOL_WINDOWS = (2, 4, 8, 16)
POOL_HALO = 16
PEER_HEADS = 8
N_KEYS = 128
PEER_TOPK = 16
VMEM_LIMIT = 56 * 1024 * 1024


def _cparams(sem):
    return pltpu.CompilerParams(dimension_semantics=sem, vmem_limit_bytes=VMEM_LIMIT)


def _rms(x, g):
    ms = jnp.mean(x * x, axis=-1, keepdims=True)
    return x * lax.rsqrt(ms + EPS) * g


def _inproj_kernel(x_ref, g_ref, w_ref, qkv_ref, zp_ref):
    h = _rms(x_ref[...], g_ref[...]).astype(BF16)
    z = jnp.dot(h, w_ref[...], preferred_element_type=F32)
    n_blocks = qkv_ref.shape[0]
    for j in range(n_blocks):
        qkv_ref[j] = z[:, j * LANES:(j + 1) * LANES]
    zp_ref[...] = z[:, n_blocks * LANES:]


def _in_proj(x2, g, w_bf, d_qkv, tm):
    t, d = x2.shape
    e = w_bf.shape[1]
    n_blocks = d_qkv // LANES
    return pl.pallas_call(
        _inproj_kernel,
        grid=(t // tm,),
        in_specs=[pl.BlockSpec((tm, d), lambda i: (i, 0)),
                  pl.BlockSpec((1, d), lambda i: (0, 0)),
                  pl.BlockSpec((d, e), lambda i: (0, 0))],
        out_specs=[pl.BlockSpec((n_blocks, tm, LANES), lambda i: (0, i, 0)),
                   pl.BlockSpec((tm, e - d_qkv), lambda i: (i, 0))],
        out_shape=[jax.ShapeDtypeStruct((n_blocks, t, LANES), F32),
                   jax.ShapeDtypeStruct((t, e - d_qkv), F32)],
        compiler_params=_cparams(("parallel",)),
        name="in_proj",
    )(x2, g, w_bf)


ATTN_TILE = DILATED_PATTERNS[-1][1] * WBLK


def _rows(ref, start, size, stride):
    if stride == 1:
        return ref[start:start + size, :]
    return ref[pl.ds(start, size, stride=stride), :]


def _attn_block(q, kk, vv1, mask2):
    lane_a = lax.broadcasted_iota(jnp.int32, q.shape, 1) < HEAD_DIM
    zero = jnp.zeros((), q.dtype)
    q2 = jnp.concatenate([jnp.where(lane_a, q, zero), jnp.where(lane_a, zero, q)], axis=0)
    s = lax.dot_general(q2, kk, (((1,), (1,)), ((), ())), preferred_element_type=F32)
    s = jnp.where(mask2, s * (HEAD_DIM ** -0.5), NEG)
    m = jnp.max(s, axis=-1, keepdims=True)
    p = jnp.exp(s - m).astype(BF16)
    pv = jnp.dot(p, vv1, preferred_element_type=F32)
    w = q.shape[0]
    num = jnp.where(lane_a, pv[:w, :LANES], pv[w:, :LANES])
    den = jnp.where(lane_a, pv[:w, LANES:], pv[w:, LANES:])
    mm = jnp.where(lane_a, m[:w], m[w:])
    return num / den, mm + jnp.log(den)


def _attn_kernel(q_ref, kp_ref, kc_ref, vp_ref, vc_ref, att_ref, o_scr, l_scr):
    n = pl.program_id(2)
    tile = q_ref.shape[0]
    qi = lax.broadcasted_iota(jnp.int32, (2 * WBLK, 2 * WBLK), 0) % WBLK
    kj = lax.broadcasted_iota(jnp.int32, (2 * WBLK, 2 * WBLK), 1)
    dist = qi + WBLK - kj
    for bi, (window, dil) in enumerate(DILATED_PATTERNS):
        band = (dist >= 0) & (dist <= window // dil)
        band_first = band & ((kj >= WBLK) | (n > 0))
        sub = tile // dil
        ones = jnp.ones((WBLK + sub, LANES), BF16)
        for c in range(dil):
            q = _rows(q_ref, c, sub, dil).astype(BF16)
            halo = tile - WBLK * dil + c
            kcat = jnp.concatenate([_rows(kp_ref, halo, WBLK, dil), _rows(kc_ref, c, sub, dil)],
                                   axis=0).astype(BF16)
            vcat = jnp.concatenate([_rows(vp_ref, halo, WBLK, dil), _rows(vc_ref, c, sub, dil)],
                                   axis=0).astype(BF16)
            vcat = jnp.concatenate([vcat, ones], axis=1)
            for j in range(sub // WBLK):
                o, lse = _attn_block(q[j * WBLK:(j + 1) * WBLK], kcat[j * WBLK:(j + 2) * WBLK],
                                     vcat[j * WBLK:(j + 2) * WBLK], band_first if j == 0 else band)
                first_row = c + dil * j * WBLK
                if dil == 1:
                    o_scr[bi, first_row:first_row + WBLK, :] = o
                    l_scr[bi, first_row:first_row + WBLK, :] = lse
                else:
                    o_scr[bi, pl.ds(first_row, WBLK, stride=dil), :] = o
                    l_scr[bi, pl.ds(first_row, WBLK, stride=dil), :] = lse

    l1, l2, l3 = l_scr[0], l_scr[1], l_scr[2]
    lm = jnp.maximum(jnp.maximum(l1, l2), l3)
    e1, e2, e3 = jnp.exp(l1 - lm), jnp.exp(l2 - lm), jnp.exp(l3 - lm)
    att_ref[...] = (e1 * o_scr[0] + e2 * o_scr[1] + e3 * o_scr[2]) / (e1 + e2 + e3)


def _attention(qkv, b, s):
    hp = qkv.shape[0] // 3
    tile = ATTN_TILE
    qkv5 = qkv.reshape(3 * hp, b, s, LANES)
    blk = (None, None, tile, LANES)

    def spec(which, prev):
        if prev:
            return pl.BlockSpec(blk, lambda bi, hi, ni: (which * hp + hi, bi, jnp.maximum(ni - 1, 0), 0))
        return pl.BlockSpec(blk, lambda bi, hi, ni: (which * hp + hi, bi, ni, 0))

    att = pl.pallas_call(
        _attn_kernel,
        grid=(b, hp, s // tile),
        in_specs=[spec(0, False), spec(1, True), spec(1, False), spec(2, True), spec(2, False)],
        out_specs=pl.BlockSpec(blk, lambda bi, hi, ni: (hi, bi, ni, 0)),
        out_shape=jax.ShapeDtypeStruct((hp, b, s, LANES), F32),
        scratch_shapes=[pltpu.VMEM((len(DILATED_PATTERNS), tile, LANES), F32),
                        pltpu.VMEM((len(DILATED_PATTERNS), tile, LANES), F32)],
        compiler_params=_cparams(("parallel", "parallel", "arbitrary")),
        name="attention",
    )(qkv5, qkv5, qkv5, qkv5, qkv5)
    return att.reshape(hp, b * s, LANES)


def _mixout_kernel(att_ref, zp_ref, zh_ref, x_ref, wp_ref, ps_ref, wo_ref, g_ref,
                   x1_ref, hf_ref, hft_ref, *, tiles_per_seq):
    i = pl.program_id(0)
    tm = x_ref.shape[0]
    first = (i % tiles_per_seq) == 0
    att = jnp.concatenate([att_ref[j] for j in range(att_ref.shape[0])], axis=1)

    zc = zp_ref[...]
    halo = jnp.where(first, 0.0, zh_ref[...])
    buf = jnp.concatenate([zc, halo], axis=0)
    sums = {1: buf}
    w = 1
    while w < POOL_WINDOWS[-1]:
        sums[2 * w] = sums[w] + pltpu.roll(sums[w], w, 0)
        w *= 2
    pos = (i % tiles_per_seq) * tm + lax.broadcasted_iota(jnp.int32, (tm, 1), 0) + 1
    cg = zc.shape[1] // len(POOL_WINDOWS)
    mixed = []
    for g, win in enumerate(POOL_WINDOWS):
        cols = slice(g * cg, (g + 1) * cg)
        cnt = jnp.minimum(pos, win).astype(F32)
        pooled = sums[win][:tm, cols] / cnt - zc[:, cols]
        mixed.append(jnp.dot(pooled.astype(BF16), wp_ref[g], preferred_element_type=F32))
    mixed = jnp.concatenate(mixed, axis=1) * ps_ref[...]

    mix = jnp.concatenate([att, mixed], axis=1).astype(BF16)
    x1 = x_ref[...] + jnp.dot(mix, wo_ref[...], preferred_element_type=F32)
    x1_ref[...] = x1
    hf = _rms(x1, g_ref[...])
    hf_ref[...] = hf.astype(BF16)
    hft_ref[...] = hf.T.astype(BF16)


def _mix_out(att, zp, x2, wp_bf, ps, wo_bf, g, seq, tm):
    t, d = x2.shape
    dp = zp.shape[1]
    row = lambda i: (i, 0)
    const2 = lambda i: (0, 0)
    halo_rows = tm // POOL_HALO
    return pl.pallas_call(
        functools.partial(_mixout_kernel, tiles_per_seq=seq // tm),
        grid=(t // tm,),
        in_specs=[
            pl.BlockSpec((att.shape[0], tm, LANES), lambda i: (0, i, 0)),
            pl.BlockSpec((tm, dp), row),
            pl.BlockSpec((POOL_HALO, dp), lambda i: (jnp.maximum(i * halo_rows - 1, 0), 0)),
            pl.BlockSpec((tm, d), row),
            pl.BlockSpec(wp_bf.shape, lambda i: (0, 0, 0)),
            pl.BlockSpec((1, dp), const2),
            pl.BlockSpec(wo_bf.shape, const2),
            pl.BlockSpec((1, d), const2)],
        out_specs=[pl.BlockSpec((tm, d), row), pl.BlockSpec((tm, d), row),
                   pl.BlockSpec((d, tm), lambda i: (0, i))],
        out_shape=[jax.ShapeDtypeStruct((t, d), F32), jax.ShapeDtypeStruct((t, d), BF16),
                   jax.ShapeDtypeStruct((d, t), BF16)],
        compiler_params=_cparams(("parallel",)),
        name="mix_out",
    )(att, zp, zp, x2, wp_bf, ps, wo_bf, g)


def _young_cells():
    return [(a, b) for a in range(PEER_TOPK) for b in range(PEER_TOPK)
            if (a + 1) * (b + 1) <= PEER_TOPK]


def _route_kernel(hf_ref, wq_ref, sk_ref, rank1_ref, e1_ref, nk_ref, e0_ref,
                  s_scr, rank_scr, val_scr, n_scr, aux_scr):
    tmr = hf_ref.shape[0]
    q = jnp.dot(hf_ref[...], wq_ref[...], preferred_element_type=F32).astype(BF16)
    for hp in range(2 * PEER_HEADS):
        s_scr[hp] = lax.dot_general(sk_ref[hp], q[:, hp * N_KEYS:(hp + 1) * N_KEYS],
                                    (((1,), (1,)), ((), ())), preferred_element_type=F32)

    no_rank = jnp.full((N_KEYS, tmr), float(PEER_TOPK), F32)
    miscount = jnp.zeros((1, tmr), F32)
    for hp in range(2 * PEER_HEADS):
        h, p = divmod(hp, 2)
        sw, rank = s_scr[hp], no_rank
        for r in range(PEER_TOPK):
            m = jnp.max(sw, axis=0, keepdims=True)
            hit = sw == m
            val_scr[p, r, h:h + 1, :] = m
            sw = jnp.where(hit, -jnp.inf, sw)
            rank = jnp.where(hit, float(r), rank)
        rank_scr[hp] = rank
        ranked = jnp.sum(jnp.where(rank < float(PEER_TOPK), 1.0, 0.0), axis=0, keepdims=True)
        miscount = jnp.maximum(miscount, jnp.abs(ranked - float(PEER_TOPK)))

    @pl.when(jnp.max(miscount) > 0.0)
    def _():
        key_id = lax.broadcasted_iota(jnp.int32, (N_KEYS, tmr), 0).astype(F32)
        for hp in range(2 * PEER_HEADS):
            h, p = divmod(hp, 2)

            def extract(r, carry):
                sw, rank = carry
                m = jnp.max(sw, axis=0, keepdims=True)
                idx = jnp.min(jnp.where(sw == m, key_id, float(N_KEYS)), axis=0, keepdims=True)
                sel = key_id == idx
                val_scr[p, r, pl.ds(h, 1), :] = m
                return jnp.where(sel, -jnp.inf, sw), jnp.where(sel, lax.convert_element_type(r, F32), rank)

            _, rank = lax.fori_loop(0, PEER_TOPK, extract, (s_scr[hp], no_rank))
            rank_scr[hp] = rank

    cells = _young_cells()
    v0 = [val_scr[0, a] for a in range(PEER_TOPK)]
    v1 = [val_scr[1, b] for b in range(PEER_TOPK)]
    csum = {c: v0[c[0]] + v1[c[1]] for c in cells}
    beaten = {c: jnp.full(csum[c].shape, float((c[0] + 1) * (c[1] + 1) - 1), F32) for c in cells}
    for ix, cx in enumerate(cells):
        for cy in cells[ix + 1:]:
            comparable = (cx[0] <= cy[0] and cx[1] <= cy[1]) or (cy[0] <= cx[0] and cy[1] <= cx[1])
            if comparable:
                continue
            y_wins = jnp.where(csum[cy] > csum[cx], 1.0, 0.0)
            beaten[cx] = beaten[cx] + y_wins
            beaten[cy] = beaten[cy] + (1.0 - y_wins)
    top = csum[(0, 0)]
    zsum = jnp.zeros_like(top)
    ncol = [jnp.zeros_like(top) for _ in range(PEER_TOPK)]
    for c in cells:
        chosen = beaten[c] < float(PEER_TOPK)
        zsum = zsum + jnp.where(chosen, jnp.exp(csum[c] - top), 0.0)
        ncol[c[0]] = ncol[c[0]] + jnp.where(chosen, 1.0, 0.0)
    for a in range(PEER_TOPK):
        n_scr[a] = ncol[a]
    aux_scr[0] = 1.0 / zsum
    aux_scr[1] = v0[0]
    aux_scr[2] = v1[0]

    for h in range(PEER_HEADS):
        rank0 = rank_scr[2 * h]
        nk = jnp.zeros((N_KEYS, tmr), F32)
        for a in range(PEER_TOPK):
            nk = jnp.where(rank0 == float(a), n_scr[a, pl.ds(h, 1), :], nk)
        nk_ref[h] = nk
        e0_ref[h] = jnp.exp(s_scr[2 * h] - aux_scr[1, pl.ds(h, 1), :]) * aux_scr[0, pl.ds(h, 1), :]
        e1_ref[h] = jnp.exp(s_scr[2 * h + 1] - aux_scr[2, pl.ds(h, 1), :]).astype(BF16)
        rank1_ref[h] = rank_scr[2 * h + 1].astype(BF16)


def _route(hf, wq_bf, sk_bf, tmr):
    t, d = hf.shape
    sk2 = sk_bf.reshape(2 * PEER_HEADS, N_KEYS, sk_bf.shape[-1])
    oshape = lambda dt: jax.ShapeDtypeStruct((PEER_HEADS, N_KEYS, t), dt)
    ospec = pl.BlockSpec((PEER_HEADS, N_KEYS, tmr), lambda i: (0, 0, i))
    return pl.pallas_call(
        _route_kernel,
        grid=(t // tmr,),
        in_specs=[pl.BlockSpec((tmr, d), lambda i: (i, 0)),
                  pl.BlockSpec(wq_bf.shape, lambda i: (0, 0)),
                  pl.BlockSpec(sk2.shape, lambda i: (0, 0, 0))],
        out_specs=[ospec] * 4,
        out_shape=[oshape(BF16), oshape(BF16), oshape(F32), oshape(F32)],
        scratch_shapes=[pltpu.VMEM((2 * PEER_HEADS, N_KEYS, tmr), F32),
                        pltpu.VMEM((2 * PEER_HEADS, N_KEYS, tmr), F32),
                        pltpu.VMEM((2, PEER_TOPK, PEER_HEADS, tmr), F32),
                        pltpu.VMEM((PEER_TOPK, PEER_HEADS, tmr), F32),
                        pltpu.VMEM((3, PEER_HEADS, tmr), F32)],
        compiler_params=_cparams(("parallel",)),
        name="route",
    )(hf, wq_bf, sk2)


PACK = 16


def _gate_rows(row_ref, h, r, tm):
    half = jnp.broadcast_to(row_ref[h, r:r + 1, :], (PACK // 2, tm))
    return jnp.concatenate([half, half], axis=0).astype(BF16)


def _peer_gate_chunk(row0, s_ref, w_ref, rank1_ref, e1_ref, nk_ref, e0_ref):
    ec, tm = s_ref.shape
    subs = N_KEYS // PACK
    for j in range(ec // N_KEYS):
        gate = [jnp.zeros((PACK, tm), BF16) for _ in range(subs)]
        for h in range(PEER_HEADS):
            n_rows = _gate_rows(nk_ref, h, row0 + j, tm)
            e0_rows = _gate_rows(e0_ref, h, row0 + j, tm)
            for k in range(subs):
                rows = slice(k * PACK, (k + 1) * PACK)
                chosen = rank1_ref[h, rows, :] < n_rows
                picked = jnp.where(chosen, e1_ref[h, rows, :], jnp.zeros((), BF16))
                gate[k] = gate[k] + picked * e0_rows
        for k in range(subs):
            rows = slice(j * N_KEYS + k * PACK, j * N_KEYS + (k + 1) * PACK)
            s = s_ref[rows, :]
            act = 0.5 * s * (1.0 + lax.erf(s * (0.5 ** 0.5)))
            w_ref[rows, :] = act.astype(BF16) * gate[k]


def _peer_kernel(u_ref, hft_ref, vt_ref, rank1_ref, e1_ref, nk_ref, e0_ref, yt_ref, s_scr, w_scr):
    c = pl.program_id(1)

    @pl.when(c == 0)
    def _():
        yt_ref[...] = jnp.zeros_like(yt_ref)

    s_scr[...] = jnp.dot(u_ref[...], hft_ref[...], preferred_element_type=F32)
    _peer_gate_chunk(0, s_scr, w_scr, rank1_ref, e1_ref, nk_ref, e0_ref)
    yt_ref[...] += jnp.dot(vt_ref[...], w_scr[...], preferred_element_type=F32)


def _peer(u_bf, hft, vt_bf, rank1, e1, nk, e0, tm, ec):
    n_exp, d = u_bf.shape
    t = hft.shape[1]
    rspec = pl.BlockSpec((PEER_HEADS, N_KEYS, tm), lambda i, c: (0, 0, i))
    rows = pl.BlockSpec((PEER_HEADS, ec // N_KEYS, tm), lambda i, c: (0, c, i))
    return pl.pallas_call(
        _peer_kernel,
        grid=(t // tm, n_exp // ec),
        in_specs=[pl.BlockSpec((ec, d), lambda i, c: (c, 0)),
                  pl.BlockSpec((d, tm), lambda i, c: (0, i)),
                  pl.BlockSpec((d, ec), lambda i, c: (0, c)),
                  rspec, rspec, rows, rows],
        out_specs=pl.BlockSpec((d, tm), lambda i, c: (0, i)),
        out_shape=jax.ShapeDtypeStruct((d, t), F32),
        scratch_shapes=[pltpu.VMEM((ec, tm), F32), pltpu.VMEM((ec, tm), BF16)],
        compiler_params=_cparams(("parallel", "arbitrary")),
        name="peer",
    )(u_bf, hft, vt_bf, rank1, e1, nk, e0)


def _final_kernel(x1_ref, yt_ref, g_ref, o_ref):
    o_ref[...] = _rms(x1_ref[...] + yt_ref[...].T, g_ref[...])


def _add_peer_kernel(x1_ref, yt_ref, o_ref):
    o_ref[...] = x1_ref[...] + yt_ref[...].T


def _add_peer(x1, yt, tm):
    t, d = x1.shape
    return pl.pallas_call(
        _add_peer_kernel,
        grid=(t // tm,),
        in_specs=[pl.BlockSpec((tm, d), lambda i: (i, 0)), pl.BlockSpec((d, tm), lambda i: (0, i))],
        out_specs=pl.BlockSpec((tm, d), lambda i: (i, 0)),
        out_shape=jax.ShapeDtypeStruct((t, d), F32),
        compiler_params=_cparams(("parallel",)),
        name="add_peer",
    )(x1, yt)


def _final(x1, yt, g, tm):
    t, d = x1.shape
    return pl.pallas_call(
        _final_kernel,
        grid=(t // tm,),
        in_specs=[pl.BlockSpec((tm, d), lambda i: (i, 0)), pl.BlockSpec((d, tm), lambda i: (0, i)),
                  pl.BlockSpec((1, d), lambda i: (0, 0))],
        out_specs=pl.BlockSpec((tm, d), lambda i: (i, 0)),
        out_shape=jax.ShapeDtypeStruct((t, d), F32),
        compiler_params=_cparams(("parallel",)),
        name="final",
    )(x1, yt, g)


def _pick(total, want):
    tile = min(total, want)
    assert total % tile == 0, (total, tile)
    return tile


def kernel(x, norm_mix, w_in, w_pool, pool_scale, w_out, norm_ffn, w_query, sub_keys,
           expert_u, expert_v, norm_final):
    b, s, d = x.shape
    t = b * s
    depth = norm_mix.shape[0]
    d_pool = w_pool.shape[1] * w_pool.shape[2]
    d_qkv = w_in.shape[2] - d_pool
    d_attn = d_qkv // 3
    hp = d_attn // LANES
    assert d_attn == w_out.shape[1] - d_pool and d_attn % LANES == 0
    assert sub_keys.shape[1:] == (PEER_HEADS, 2, N_KEYS, N_KEYS)
    assert expert_u.shape[1] == N_KEYS * N_KEYS
    assert s % (DILATED_PATTERNS[-1][1] * WBLK) == 0

    tm = _pick(s, 512)
    tm_route = _pick(s, 128)
    tm_peer = _pick(s, 1024)
    ec = 1024

    x2 = x.reshape(t, d)
    for layer in range(depth):
        qkv, zp = _in_proj(x2, norm_mix[layer][None], w_in[layer].astype(BF16), d_qkv, tm)
        att = _attention(qkv, b, s)
        x1, hf, hft = _mix_out(att, zp, x2, w_pool[layer].astype(BF16),
                               pool_scale[layer][None], w_out[layer].astype(BF16),
                               norm_ffn[layer][None], s, tm)
        rank1, e1, nk, e0 = _route(hf, w_query[layer].astype(BF16), sub_keys[layer].astype(BF16),
                                   tm_route)
        yt = _peer(expert_u[layer].astype(BF16), hft, expert_v[layer].astype(BF16).T,
                   rank1, e1, nk, e0, tm_peer, ec)
        if layer + 1 < depth:
            x2 = _add_peer(x1, yt, tm)
    out = _final(x1, yt, norm_final[None], tm)
    return out.reshape(b, s, d)
```

```python
import functools

import jax
import jax.numpy as jnp
from jax import lax
from jax.experimental import pallas as pl
from jax.experimental.pallas import tpu as pltpu

F32 = jnp.float32
BF16 = jnp.bfloat16

EPS = 1e-6
NEG = -1e30
HEAD_DIM = 64
LANES = 128
DILATED_PATTERNS = ((128, 1), (512, 4), (2048, 16))
WBLK = 128
POOL_WINDOWS = (2, 4, 8, 16)
POOL_HALO = 16
PEER_HEADS = 8
N_KEYS = 128
PEER_TOPK = 16
VMEM_LIMIT = 56 * 1024 * 1024


def _cparams(sem):
    return pltpu.CompilerParams(dimension_semantics=sem, vmem_limit_bytes=VMEM_LIMIT)


def _rms(x, g):
    ms = jnp.mean(x * x, axis=-1, keepdims=True)
    return x * lax.rsqrt(ms + EPS) * g


def _inproj_kernel(x_ref, g_ref, w_ref, qkv_ref, zp_ref):
    h = _rms(x_ref[...], g_ref[...]).astype(BF16)
    z = jnp.dot(h, w_ref[...], preferred_element_type=F32)
    n_blocks = qkv_ref.shape[0]
    for j in range(n_blocks):
        qkv_ref[j] = z[:, j * LANES:(j + 1) * LANES]
    zp_ref[...] = z[:, n_blocks * LANES:]


def _in_proj(x2, g, w_bf, d_qkv, tm):
    t, d = x2.shape
    e = w_bf.shape[1]
    n_blocks = d_qkv // LANES
    return pl.pallas_call(
        _inproj_kernel,
        grid=(t // tm,),
        in_specs=[pl.BlockSpec((tm, d), lambda i: (i, 0)),
                  pl.BlockSpec((1, d), lambda i: (0, 0)),
                  pl.BlockSpec((d, e), lambda i: (0, 0))],
        out_specs=[pl.BlockSpec((n_blocks, tm, LANES), lambda i: (0, i, 0)),
                   pl.BlockSpec((tm, e - d_qkv), lambda i: (i, 0))],
        out_shape=[jax.ShapeDtypeStruct((n_blocks, t, LANES), F32),
                   jax.ShapeDtypeStruct((t, e - d_qkv), F32)],
        compiler_params=_cparams(("parallel",)),
        name="in_proj",
    )(x2, g, w_bf)


ATTN_TILE = DILATED_PATTERNS[-1][1] * WBLK


def _rows(ref, start, size, stride):
    if stride == 1:
        return ref[start:start + size, :]
    return ref[pl.ds(start, size, stride=stride), :]


def _attn_block(q, kk, vv1, mask2):
    lane_a = lax.broadcasted_iota(jnp.int32, q.shape, 1) < HEAD_DIM
    zero = jnp.zeros((), q.dtype)
    q2 = jnp.concatenate([jnp.where(lane_a, q, zero), jnp.where(lane_a, zero, q)], axis=0)
    s = lax.dot_general(q2, kk, (((1,), (1,)), ((), ())), preferred_element_type=F32)
    s = jnp.where(mask2, s * (HEAD_DIM ** -0.5), NEG)
    m = jnp.max(s, axis=-1, keepdims=True)
    p = jnp.exp(s - m).astype(BF16)
    pv = jnp.dot(p, vv1, preferred_element_type=F32)
    w = q.shape[0]
    num = jnp.where(lane_a, pv[:w, :LANES], pv[w:, :LANES])
    den = jnp.where(lane_a, pv[:w, LANES:], pv[w:, LANES:])
    mm = jnp.where(lane_a, m[:w], m[w:])
    return num / den, mm + jnp.log(den)


def _attn_kernel(q_ref, kp_ref, kc_ref, vp_ref, vc_ref, att_ref, o_scr, l_scr):
    n = pl.program_id(2)
    tile = q_ref.shape[0]
    qi = lax.broadcasted_iota(jnp.int32, (2 * WBLK, 2 * WBLK), 0) % WBLK
    kj = lax.broadcasted_iota(jnp.int32, (2 * WBLK, 2 * WBLK), 1)
    dist = qi + WBLK - kj
    for bi, (window, dil) in enumerate(DILATED_PATTERNS):
        band = (dist >= 0) & (dist <= window // dil)
        band_first = band & ((kj >= WBLK) | (n > 0))
        sub = tile // dil
        ones = jnp.ones((WBLK + sub, LANES), BF16)
        for c in range(dil):
            q = _rows(q_ref, c, sub, dil).astype(BF16)
            halo = tile - WBLK * dil + c
            kcat = jnp.concatenate([_rows(kp_ref, halo, WBLK, dil), _rows(kc_ref, c, sub, dil)],
                                   axis=0).astype(BF16)
            vcat = jnp.concatenate([_rows(vp_ref, halo, WBLK, dil), _rows(vc_ref, c, sub, dil)],
                                   axis=0).astype(BF16)
            vcat = jnp.concatenate([vcat, ones], axis=1)
            for j in range(sub // WBLK):
                o, lse = _attn_block(q[j * WBLK:(j + 1) * WBLK], kcat[j * WBLK:(j + 2) * WBLK],
                                     vcat[j * WBLK:(j + 2) * WBLK], band_first if j == 0 else band)
                first_row = c + dil * j * WBLK
                if dil == 1:
                    o_scr[bi, first_row:first_row + WBLK, :] = o
                    l_scr[bi, first_row:first_row + WBLK, :] = lse
                else:
                    o_scr[bi, pl.ds(first_row, WBLK, stride=dil), :] = o
                    l_scr[bi, pl.ds(first_row, WBLK, stride=dil), :] = lse

    l1, l2, l3 = l_scr[0], l_scr[1], l_scr[2]
    lm = jnp.maximum(jnp.maximum(l1, l2), l3)
    e1, e2, e3 = jnp.exp(l1 - lm), jnp.exp(l2 - lm), jnp.exp(l3 - lm)
    att_ref[...] = (e1 * o_scr[0] + e2 * o_scr[1] + e3 * o_scr[2]) / (e1 + e2 + e3)


def _attention(qkv, b, s):
    hp = qkv.shape[0] // 3
    tile = ATTN_TILE
    qkv5 = qkv.reshape(3 * hp, b, s, LANES)
    blk = (None, None, tile, LANES)

    def spec(which, prev):
        if prev:
            return pl.BlockSpec(blk, lambda bi, hi, ni: (which * hp + hi, bi, jnp.maximum(ni - 1, 0), 0))
        return pl.BlockSpec(blk, lambda bi, hi, ni: (which * hp + hi, bi, ni, 0))

    att = pl.pallas_call(
        _attn_kernel,
        grid=(b, hp, s // tile),
        in_specs=[spec(0, False), spec(1, True), spec(1, False), spec(2, True), spec(2, False)],
        out_specs=pl.BlockSpec(blk, lambda bi, hi, ni: (hi, bi, ni, 0)),
        out_shape=jax.ShapeDtypeStruct((hp, b, s, LANES), F32),
        scratch_shapes=[pltpu.VMEM((len(DILATED_PATTERNS), tile, LANES), F32),
                        pltpu.VMEM((len(DILATED_PATTERNS), tile, LANES), F32)],
        compiler_params=_cparams(("parallel", "parallel", "arbitrary")),
        name="attention",
    )(qkv5, qkv5, qkv5, qkv5, qkv5)
    return att.reshape(hp, b * s, LANES)


def _mixout_kernel(att_ref, zp_ref, zh_ref, x_ref, wp_ref, ps_ref, wo_ref, g_ref,
                   x1_ref, hf_ref, hft_ref, *, tiles_per_seq):
    i = pl.program_id(0)
    tm = x_ref.shape[0]
    first = (i % tiles_per_seq) == 0
    att = jnp.concatenate([att_ref[j] for j in range(att_ref.shape[0])], axis=1)

    zc = zp_ref[...]
    halo = jnp.where(first, 0.0, zh_ref[...])
    buf = jnp.concatenate([zc, halo], axis=0)
    sums = {1: buf}
    w = 1
    while w < POOL_WINDOWS[-1]:
        sums[2 * w] = sums[w] + pltpu.roll(sums[w], w, 0)
        w *= 2
    pos = (i % tiles_per_seq) * tm + lax.broadcasted_iota(jnp.int32, (tm, 1), 0) + 1
    cg = zc.shape[1] // len(POOL_WINDOWS)
    mixed = []
    for g, win in enumerate(POOL_WINDOWS):
        cols = slice(g * cg, (g + 1) * cg)
        cnt = jnp.minimum(pos, win).astype(F32)
        pooled = sums[win][:tm, cols] / cnt - zc[:, cols]
        mixed.append(jnp.dot(pooled.astype(BF16), wp_ref[g], preferred_element_type=F32))
    mixed = jnp.concatenate(mixed, axis=1) * ps_ref[...]

    mix = jnp.concatenate([att, mixed], axis=1).astype(BF16)
    x1 = x_ref[...] + jnp.dot(mix, wo_ref[...], preferred_element_type=F32)
    x1_ref[...] = x1
    hf = _rms(x1, g_ref[...])
    hf_ref[...] = hf.astype(BF16)
    hft_ref[...] = hf.T.astype(BF16)


def _mix_out(att, zp, x2, wp_bf, ps, wo_bf, g, seq, tm):
    t, d = x2.shape
    dp = zp.shape[1]
    row = lambda i: (i, 0)
    const2 = lambda i: (0, 0)
    halo_rows = tm // POOL_HALO
    return pl.pallas_call(
        functools.partial(_mixout_kernel, tiles_per_seq=seq // tm),
        grid=(t // tm,),
        in_specs=[
            pl.BlockSpec((att.shape[0], tm, LANES), lambda i: (0, i, 0)),
            pl.BlockSpec((tm, dp), row),
            pl.BlockSpec((POOL_HALO, dp), lambda i: (jnp.maximum(i * halo_rows - 1, 0), 0)),
            pl.BlockSpec((tm, d), row),
            pl.BlockSpec(wp_bf.shape, lambda i: (0, 0, 0)),
            pl.BlockSpec((1, dp), const2),
            pl.BlockSpec(wo_bf.shape, const2),
            pl.BlockSpec((1, d), const2)],
        out_specs=[pl.BlockSpec((tm, d), row), pl.BlockSpec((tm, d), row),
                   pl.BlockSpec((d, tm), lambda i: (0, i))],
        out_shape=[jax.ShapeDtypeStruct((t, d), F32), jax.ShapeDtypeStruct((t, d), BF16),
                   jax.ShapeDtypeStruct((d, t), BF16)],
        compiler_params=_cparams(("parallel",)),
        name="mix_out",
    )(att, zp, zp, x2, wp_bf, ps, wo_bf, g)


def _young_cells():
    return [(a, b) for a in range(PEER_TOPK) for b in range(PEER_TOPK)
            if (a + 1) * (b + 1) <= PEER_TOPK]


def _route_kernel(hf_ref, wq_ref, sk_ref, rank1_ref, e1_ref, nk_ref, e0_ref,
                  s_scr, rank_scr, val_scr, n_scr, aux_scr):
    tmr = hf_ref.shape[0]
    q = jnp.dot(hf_ref[...], wq_ref[...], preferred_element_type=F32).astype(BF16)
    for hp in range(2 * PEER_HEADS):
        s_scr[hp] = lax.dot_general(sk_ref[hp], q[:, hp * N_KEYS:(hp + 1) * N_KEYS],
                                    (((1,), (1,)), ((), ())), preferred_element_type=F32)

    no_rank = jnp.full((N_KEYS, tmr), float(PEER_TOPK), F32)
    miscount = jnp.zeros((1, tmr), F32)
    for hp in range(2 * PEER_HEADS):
        h, p = divmod(hp, 2)
        sw, rank = s_scr[hp], no_rank
        for r in range(PEER_TOPK):
            m = jnp.max(sw, axis=0, keepdims=True)
            hit = sw == m
            val_scr[p, r, h:h + 1, :] = m
            sw = jnp.where(hit, -jnp.inf, sw)
            rank = jnp.where(hit, float(r), rank)
        rank_scr[hp] = rank
        ranked = jnp.sum(jnp.where(rank < float(PEER_TOPK), 1.0, 0.0), axis=0, keepdims=True)
        miscount = jnp.maximum(miscount, jnp.abs(ranked - float(PEER_TOPK)))

    @pl.when(jnp.max(miscount) > 0.0)
    def _():
        key_id = lax.broadcasted_iota(jnp.int32, (N_KEYS, tmr), 0).astype(F32)
        for hp in range(2 * PEER_HEADS):
            h, p = divmod(hp, 2)

            def extract(r, carry):
                sw, rank = carry
                m = jnp.max(sw, axis=0, keepdims=True)
                idx = jnp.min(jnp.where(sw == m, key_id, float(N_KEYS)), axis=0, keepdims=True)
                sel = key_id == idx
                val_scr[p, r, pl.ds(h, 1), :] = m
                return jnp.where(sel, -jnp.inf, sw), jnp.where(sel, lax.convert_element_type(r, F32), rank)

            _, rank = lax.fori_loop(0, PEER_TOPK, extract, (s_scr[hp], no_rank))
            rank_scr[hp] = rank

    cells = _young_cells()
    v0 = [val_scr[0, a] for a in range(PEER_TOPK)]
    v1 = [val_scr[1, b] for b in range(PEER_TOPK)]
    csum = {c: v0[c[0]] + v1[c[1]] for c in cells}
    beaten = {c: jnp.full(csum[c].shape, float((c[0] + 1) * (c[1] + 1) - 1), F32) for c in cells}
    for ix, cx in enumerate(cells):
        for cy in cells[ix + 1:]:
            comparable = (cx[0] <= cy[0] and cx[1] <= cy[1]) or (cy[0] <= cx[0] and cy[1] <= cx[1])
            if comparable:
                continue
            y_wins = jnp.where(csum[cy] > csum[cx], 1.0, 0.0)
            beaten[cx] = beaten[cx] + y_wins
            beaten[cy] = beaten[cy] + (1.0 - y_wins)
    top = csum[(0, 0)]
    zsum = jnp.zeros_like(top)
    ncol = [jnp.zeros_like(top) for _ in range(PEER_TOPK)]
    for c in cells:
        chosen = beaten[c] < float(PEER_TOPK)
        zsum = zsum + jnp.where(chosen, jnp.exp(csum[c] - top), 0.0)
        ncol[c[0]] = ncol[c[0]] + jnp.where(chosen, 1.0, 0.0)
    for a in range(PEER_TOPK):
        n_scr[a] = ncol[a]
    aux_scr[0] = 1.0 / zsum
    aux_scr[1] = v0[0]
    aux_scr[2] = v1[0]

    for h in range(PEER_HEADS):
        rank0 = rank_scr[2 * h]
        nk = jnp.zeros((N_KEYS, tmr), F32)
        for a in range(PEER_TOPK):
            nk = jnp.where(rank0 == float(a), n_scr[a, pl.ds(h, 1), :], nk)
        nk_ref[h] = nk
        e0_ref[h] = jnp.exp(s_scr[2 * h] - aux_scr[1, pl.ds(h, 1), :]) * aux_scr[0, pl.ds(h, 1), :]
        e1_ref[h] = jnp.exp(s_scr[2 * h + 1] - aux_scr[2, pl.ds(h, 1), :]).astype(BF16)
        rank1_ref[h] = rank_scr[2 * h + 1].astype(BF16)


def _route(hf, wq_bf, sk_bf, tmr):
    t, d = hf.shape
    sk2 = sk_bf.reshape(2 * PEER_HEADS, N_KEYS, sk_bf.shape[-1])
    oshape = lambda dt: jax.ShapeDtypeStruct((PEER_HEADS, N_KEYS, t), dt)
    ospec = pl.BlockSpec((PEER_HEADS, N_KEYS, tmr), lambda i: (0, 0, i))
    return pl.pallas_call(
        _route_kernel,
        grid=(t // tmr,),
        in_specs=[pl.BlockSpec((tmr, d), lambda i: (i, 0)),
                  pl.BlockSpec(wq_bf.shape, lambda i: (0, 0)),
                  pl.BlockSpec(sk2.shape, lambda i: (0, 0, 0))],
        out_specs=[ospec] * 4,
        out_shape=[oshape(BF16), oshape(BF16), oshape(F32), oshape(F32)],
        scratch_shapes=[pltpu.VMEM((2 * PEER_HEADS, N_KEYS, tmr), F32),
                        pltpu.VMEM((2 * PEER_HEADS, N_KEYS, tmr), F32),
                        pltpu.VMEM((2, PEER_TOPK, PEER_HEADS, tmr), F32),
                        pltpu.VMEM((PEER_TOPK, PEER_HEADS, tmr), F32),
                        pltpu.VMEM((3, PEER_HEADS, tmr), F32)],
        compiler_params=_cparams(("parallel",)),
        name="route",
    )(hf, wq_bf, sk2)


PACK = 16


GATE_LANES = 256


def _gate_rows(row_ref, h, r, lanes):
    row = row_ref[h, r:r + 1, lanes]
    half = jnp.broadcast_to(row, (PACK // 2, row.shape[1]))
    return jnp.concatenate([half, half], axis=0).astype(BF16)


def _peer_gate_chunk(row0, s_ref, w_ref, rank1_ref, e1_ref, nk_ref, e0_ref):
    ec, tm = s_ref.shape
    subs = N_KEYS // PACK
    for j in range(ec // N_KEYS):
        for g in range(tm // GATE_LANES):
            lanes = slice(g * GATE_LANES, (g + 1) * GATE_LANES)
            gate = [jnp.zeros((PACK, GATE_LANES), BF16) for _ in range(subs)]
            for h in range(PEER_HEADS):
                n_rows = _gate_rows(nk_ref, h, row0 + j, lanes)
                e0_rows = _gate_rows(e0_ref, h, row0 + j, lanes)
                for k in range(subs):
                    rows = slice(k * PACK, (k + 1) * PACK)
                    chosen = rank1_ref[h, rows, lanes] < n_rows
                    picked = jnp.where(chosen, e1_ref[h, rows, lanes], jnp.zeros((), BF16))
                    gate[k] = gate[k] + picked * e0_rows
            for k in range(subs):
                rows = slice(j * N_KEYS + k * PACK, j * N_KEYS + (k + 1) * PACK)
                s = s_ref[rows, lanes].astype(BF16)
                act = 0.5 * s * (1.0 + lax.erf(s * (0.5 ** 0.5)))
                w_ref[rows, lanes] = act * gate[k]


def _peer_kernel(u_ref, hft_ref, vt_ref, rank1_ref, e1_ref, nk_ref, e0_ref, yt_ref, s_scr, w_scr):
    c = pl.program_id(1)

    @pl.when(c == 0)
    def _():
        yt_ref[...] = jnp.zeros_like(yt_ref)

    s_scr[...] = jnp.dot(u_ref[...], hft_ref[...], preferred_element_type=F32)
    _peer_gate_chunk(0, s_scr, w_scr, rank1_ref, e1_ref, nk_ref, e0_ref)
    yt_ref[...] += jnp.dot(vt_ref[...], w_scr[...], preferred_element_type=F32)


def _peer(u_bf, hft, vt_bf, rank1, e1, nk, e0, tm, ec):
    n_exp, d = u_bf.shape
    t = hft.shape[1]
    rspec = pl.BlockSpec((PEER_HEADS, N_KEYS, tm), lambda i, c: (0, 0, i))
    rows = pl.BlockSpec((PEER_HEADS, ec // N_KEYS, tm), lambda i, c: (0, c, i))
    return pl.pallas_call(
        _peer_kernel,
        grid=(t // tm, n_exp // ec),
        in_specs=[pl.BlockSpec((ec, d), lambda i, c: (c, 0)),
                  pl.BlockSpec((d, tm), lambda i, c: (0, i)),
                  pl.BlockSpec((d, ec), lambda i, c: (0, c)),
                  rspec, rspec, rows, rows],
        out_specs=pl.BlockSpec((d, tm), lambda i, c: (0, i)),
        out_shape=jax.ShapeDtypeStruct((d, t), F32),
        scratch_shapes=[pltpu.VMEM((ec, tm), F32), pltpu.VMEM((ec, tm), BF16)],
        compiler_params=_cparams(("parallel", "arbitrary")),
        name="peer",
    )(u_bf, hft, vt_bf, rank1, e1, nk, e0)


def _final_kernel(x1_ref, yt_ref, g_ref, o_ref):
    o_ref[...] = _rms(x1_ref[...] + yt_ref[...].T, g_ref[...])


def _add_peer_kernel(x1_ref, yt_ref, o_ref):
    o_ref[...] = x1_ref[...] + yt_ref[...].T


def _add_peer(x1, yt, tm):
    t, d = x1.shape
    return pl.pallas_call(
        _add_peer_kernel,
        grid=(t // tm,),
        in_specs=[pl.BlockSpec((tm, d), lambda i: (i, 0)), pl.BlockSpec((d, tm), lambda i: (0, i))],
        out_specs=pl.BlockSpec((tm, d), lambda i: (i, 0)),
        out_shape=jax.ShapeDtypeStruct((t, d), F32),
        compiler_params=_cparams(("parallel",)),
        name="add_peer",
    )(x1, yt)


def _final(x1, yt, g, tm):
    t, d = x1.shape
    return pl.pallas_call(
        _final_kernel,
        grid=(t // tm,),
        in_specs=[pl.BlockSpec((tm, d), lambda i: (i, 0)), pl.BlockSpec((d, tm), lambda i: (0, i)),
                  pl.BlockSpec((1, d), lambda i: (0, 0))],
        out_specs=pl.BlockSpec((tm, d), lambda i: (i, 0)),
        out_shape=jax.ShapeDtypeStruct((t, d), F32),
        compiler_params=_cparams(("parallel",)),
        name="final",
    )(x1, yt, g)


def _pick(total, want):
    tile = min(total, want)
    assert total % tile == 0, (total, tile)
    return tile


def kernel(x, norm_mix, w_in, w_pool, pool_scale, w_out, norm_ffn, w_query, sub_keys,
           expert_u, expert_v, norm_final):
    b, s, d = x.shape
    t = b * s
    depth = norm_mix.shape[0]
    d_pool = w_pool.shape[1] * w_pool.shape[2]
    d_qkv = w_in.shape[2] - d_pool
    d_attn = d_qkv // 3
    hp = d_attn // LANES
    assert d_attn == w_out.shape[1] - d_pool and d_attn % LANES == 0
    assert sub_keys.shape[1:] == (PEER_HEADS, 2, N_KEYS, N_KEYS)
    assert expert_u.shape[1] == N_KEYS * N_KEYS
    assert s % (DILATED_PATTERNS[-1][1] * WBLK) == 0

    tm = _pick(s, 512)
    tm_route = _pick(s, 128)
    tm_peer = _pick(s, 1024)
    ec = 1024

    x2 = x.reshape(t, d)
    for layer in range(depth):
        qkv, zp = _in_proj(x2, norm_mix[layer][None], w_in[layer].astype(BF16), d_qkv, tm)
        att = _attention(qkv, b, s)
        x1, hf, hft = _mix_out(att, zp, x2, w_pool[layer].astype(BF16),
                               pool_scale[layer][None], w_out[layer].astype(BF16),
                               norm_ffn[layer][None], s, tm)
        rank1, e1, nk, e0 = _route(hf, w_query[layer].astype(BF16), sub_keys[layer].astype(BF16),
                                   tm_route)
        yt = _peer(expert_u[layer].astype(BF16), hft, expert_v[layer].astype(BF16).T,
                   rank1, e1, nk, e0, tm_peer, ec)
        if layer + 1 < depth:
            x2 = _add_peer(x1, yt, tm)
    out = _final(x1, yt, norm_final[None], tm)
    return out.reshape(b, s, d)
```

```python
import functools

import jax
import jax.numpy as jnp
from jax import lax
from jax.experimental import pallas as pl
from jax.experimental.pallas import tpu as pltpu

F32 = jnp.float32
BF16 = jnp.bfloat16

EPS = 1e-6
NEG = -1e30
HEAD_DIM = 64
LANES = 128
DILATED_PATTERNS = ((128, 1), (512, 4), (2048, 16))
WBLK = 128
POOL_WINDOWS = (2, 4, 8, 16)
POOL_HALO = 16
PEER_HEADS = 8
N_KEYS = 128
PEER_TOPK = 16
VMEM_LIMIT = 56 * 1024 * 1024


def _cparams(sem):
    return pltpu.CompilerParams(dimension_semantics=sem, vmem_limit_bytes=VMEM_LIMIT)


def _rms(x, g):
    ms = jnp.mean(x * x, axis=-1, keepdims=True)
    return x * lax.rsqrt(ms + EPS) * g


def _inproj_kernel(x_ref, g_ref, w_ref, qkv_ref, zp_ref):
    h = _rms(x_ref[...], g_ref[...]).astype(BF16)
    z = jnp.dot(h, w_ref[...], preferred_element_type=F32)
    n_blocks = qkv_ref.shape[0]
    for j in range(n_blocks):
        qkv_ref[j] = z[:, j * LANES:(j + 1) * LANES]
    zp_ref[...] = z[:, n_blocks * LANES:]


def _in_proj(x2, g, w_bf, d_qkv, tm):
    t, d = x2.shape
    e = w_bf.shape[1]
    n_blocks = d_qkv // LANES
    return pl.pallas_call(
        _inproj_kernel,
        grid=(t // tm,),
        in_specs=[pl.BlockSpec((tm, d), lambda i: (i, 0)),
                  pl.BlockSpec((1, d), lambda i: (0, 0)),
                  pl.BlockSpec((d, e), lambda i: (0, 0))],
        out_specs=[pl.BlockSpec((n_blocks, tm, LANES), lambda i: (0, i, 0)),
                   pl.BlockSpec((tm, e - d_qkv), lambda i: (i, 0))],
        out_shape=[jax.ShapeDtypeStruct((n_blocks, t, LANES), F32),
                   jax.ShapeDtypeStruct((t, e - d_qkv), F32)],
        compiler_params=_cparams(("parallel",)),
        name="in_proj",
    )(x2, g, w_bf)


ATTN_TILE = DILATED_PATTERNS[-1][1] * WBLK


def _rows(ref, start, size, stride):
    if stride == 1:
        return ref[start:start + size, :]
    return ref[pl.ds(start, size, stride=stride), :]


def _attn_block(q, kk, vv1, mask2):
    lane_a = lax.broadcasted_iota(jnp.int32, q.shape, 1) < HEAD_DIM
    zero = jnp.zeros((), q.dtype)
    q2 = jnp.concatenate([jnp.where(lane_a, q, zero), jnp.where(lane_a, zero, q)], axis=0)
    s = lax.dot_general(q2, kk, (((1,), (1,)), ((), ())), preferred_element_type=F32)
    s = jnp.where(mask2, s * (HEAD_DIM ** -0.5), NEG)
    m = jnp.max(s, axis=-1, keepdims=True)
    p = jnp.exp(s - m).astype(BF16)
    pv = jnp.dot(p, vv1, preferred_element_type=F32)
    w = q.shape[0]
    num = jnp.where(lane_a, pv[:w, :LANES], pv[w:, :LANES])
    den = jnp.where(lane_a, pv[:w, LANES:], pv[w:, LANES:])
    mm = jnp.where(lane_a, m[:w], m[w:])
    return num / den, mm + jnp.log(den)


def _attn_kernel(q_ref, kp_ref, kc_ref, vp_ref, vc_ref, att_ref, o_scr, l_scr):
    n = pl.program_id(2)
    tile = q_ref.shape[0]
    qi = lax.broadcasted_iota(jnp.int32, (2 * WBLK, 2 * WBLK), 0) % WBLK
    kj = lax.broadcasted_iota(jnp.int32, (2 * WBLK, 2 * WBLK), 1)
    dist = qi + WBLK - kj
    for bi, (window, dil) in enumerate(DILATED_PATTERNS):
        band = (dist >= 0) & (dist <= window // dil)
        band_first = band & ((kj >= WBLK) | (n > 0))
        sub = tile // dil
        ones = jnp.ones((WBLK + sub, LANES), BF16)
        for c in range(dil):
            q = _rows(q_ref, c, sub, dil).astype(BF16)
            halo = tile - WBLK * dil + c
            kcat = jnp.concatenate([_rows(kp_ref, halo, WBLK, dil), _rows(kc_ref, c, sub, dil)],
                                   axis=0).astype(BF16)
            vcat = jnp.concatenate([_rows(vp_ref, halo, WBLK, dil), _rows(vc_ref, c, sub, dil)],
                                   axis=0).astype(BF16)
            vcat = jnp.concatenate([vcat, ones], axis=1)
            for j in range(sub // WBLK):
                o, lse = _attn_block(q[j * WBLK:(j + 1) * WBLK], kcat[j * WBLK:(j + 2) * WBLK],
                                     vcat[j * WBLK:(j + 2) * WBLK], band_first if j == 0 else band)
                first_row = c + dil * j * WBLK
                if dil == 1:
                    o_scr[bi, first_row:first_row + WBLK, :] = o
                    l_scr[bi, first_row:first_row + WBLK, :] = lse
                else:
                    o_scr[bi, pl.ds(first_row, WBLK, stride=dil), :] = o
                    l_scr[bi, pl.ds(first_row, WBLK, stride=dil), :] = lse

    l1, l2, l3 = l_scr[0], l_scr[1], l_scr[2]
    lm = jnp.maximum(jnp.maximum(l1, l2), l3)
    e1, e2, e3 = jnp.exp(l1 - lm), jnp.exp(l2 - lm), jnp.exp(l3 - lm)
    att_ref[...] = (e1 * o_scr[0] + e2 * o_scr[1] + e3 * o_scr[2]) / (e1 + e2 + e3)


def _attention(qkv, b, s):
    hp = qkv.shape[0] // 3
    tile = ATTN_TILE
    qkv5 = qkv.reshape(3 * hp, b, s, LANES)
    blk = (None, None, tile, LANES)

    def spec(which, prev):
        if prev:
            return pl.BlockSpec(blk, lambda bi, hi, ni: (which * hp + hi, bi, jnp.maximum(ni - 1, 0), 0))
        return pl.BlockSpec(blk, lambda bi, hi, ni: (which * hp + hi, bi, ni, 0))

    att = pl.pallas_call(
        _attn_kernel,
        grid=(b, hp, s // tile),
        in_specs=[spec(0, False), spec(1, True), spec(1, False), spec(2, True), spec(2, False)],
        out_specs=pl.BlockSpec(blk, lambda bi, hi, ni: (hi, bi, ni, 0)),
        out_shape=jax.ShapeDtypeStruct((hp, b, s, LANES), F32),
        scratch_shapes=[pltpu.VMEM((len(DILATED_PATTERNS), tile, LANES), F32),
                        pltpu.VMEM((len(DILATED_PATTERNS), tile, LANES), F32)],
        compiler_params=_cparams(("parallel", "parallel", "arbitrary")),
        name="attention",
    )(qkv5, qkv5, qkv5, qkv5, qkv5)
    return att.reshape(hp, b * s, LANES)


def _mixout_kernel(att_ref, zp_ref, zh_ref, x_ref, wp_ref, ps_ref, wo_ref, g_ref,
                   x1_ref, hf_ref, hft_ref, *, tiles_per_seq):
    i = pl.program_id(0)
    tm = x_ref.shape[0]
    first = (i % tiles_per_seq) == 0
    att = jnp.concatenate([att_ref[j] for j in range(att_ref.shape[0])], axis=1)

    zc = zp_ref[...]
    halo = jnp.where(first, 0.0, zh_ref[...])
    buf = jnp.concatenate([zc, halo], axis=0)
    sums = {1: buf}
    w = 1
    while w < POOL_WINDOWS[-1]:
        sums[2 * w] = sums[w] + pltpu.roll(sums[w], w, 0)
        w *= 2
    pos = (i % tiles_per_seq) * tm + lax.broadcasted_iota(jnp.int32, (tm, 1), 0) + 1
    cg = zc.shape[1] // len(POOL_WINDOWS)
    mixed = []
    for g, win in enumerate(POOL_WINDOWS):
        cols = slice(g * cg, (g + 1) * cg)
        cnt = jnp.minimum(pos, win).astype(F32)
        pooled = sums[win][:tm, cols] / cnt - zc[:, cols]
        mixed.append(jnp.dot(pooled.astype(BF16), wp_ref[g], preferred_element_type=F32))
    mixed = jnp.concatenate(mixed, axis=1) * ps_ref[...]

    mix = jnp.concatenate([att, mixed], axis=1).astype(BF16)
    x1 = x_ref[...] + jnp.dot(mix, wo_ref[...], preferred_element_type=F32)
    x1_ref[...] = x1
    hf = _rms(x1, g_ref[...])
    hf_ref[...] = hf.astype(BF16)
    hft_ref[...] = hf.T.astype(BF16)


def _mix_out(att, zp, x2, wp_bf, ps, wo_bf, g, seq, tm):
    t, d = x2.shape
    dp = zp.shape[1]
    row = lambda i: (i, 0)
    const2 = lambda i: (0, 0)
    halo_rows = tm // POOL_HALO
    return pl.pallas_call(
        functools.partial(_mixout_kernel, tiles_per_seq=seq // tm),
        grid=(t // tm,),
        in_specs=[
            pl.BlockSpec((att.shape[0], tm, LANES), lambda i: (0, i, 0)),
            pl.BlockSpec((tm, dp), row),
            pl.BlockSpec((POOL_HALO, dp), lambda i: (jnp.maximum(i * halo_rows - 1, 0), 0)),
            pl.BlockSpec((tm, d), row),
            pl.BlockSpec(wp_bf.shape, lambda i: (0, 0, 0)),
            pl.BlockSpec((1, dp), const2),
            pl.BlockSpec(wo_bf.shape, const2),
            pl.BlockSpec((1, d), const2)],
        out_specs=[pl.BlockSpec((tm, d), row), pl.BlockSpec((tm, d), row),
                   pl.BlockSpec((d, tm), lambda i: (0, i))],
        out_shape=[jax.ShapeDtypeStruct((t, d), F32), jax.ShapeDtypeStruct((t, d), BF16),
                   jax.ShapeDtypeStruct((d, t), BF16)],
        compiler_params=_cparams(("parallel",)),
        name="mix_out",
    )(att, zp, zp, x2, wp_bf, ps, wo_bf, g)


def _young_cells():
    return [(a, b) for a in range(PEER_TOPK) for b in range(PEER_TOPK)
            if (a + 1) * (b + 1) <= PEER_TOPK]


def _sort16_pairs():
    n, pairs, p = PEER_TOPK, [], 1
    while p < n:
        k = p
        while k >= 1:
            for j in range(k % p, n - k, 2 * k):
                for i in range(min(k, n - j - k)):
                    if (i + j) // (2 * p) == (i + j + k) // (2 * p):
                        pairs.append((i + j, i + j + k))
            k //= 2
        p *= 2
    return pairs


def _top16_sorted(vals):
    groups = []
    for g in range(0, len(vals), PEER_TOPK):
        x = list(vals[g:g + PEER_TOPK])
        for a, b in _sort16_pairs():
            x[a], x[b] = jnp.maximum(x[a], x[b]), jnp.minimum(x[a], x[b])
        groups.append(x)
    while len(groups) > 1:
        merged = []
        for x, y in zip(groups[0::2], groups[1::2]):
            z = [jnp.maximum(x[i], y[PEER_TOPK - 1 - i]) for i in range(PEER_TOPK)]
            d = PEER_TOPK // 2
            while d >= 1:
                for i in range(PEER_TOPK):
                    if i & d == 0:
                        z[i], z[i + d] = jnp.maximum(z[i], z[i + d]), jnp.minimum(z[i], z[i + d])
                d //= 2
            merged.append(z)
        groups = merged
    return groups[0]


def _count_greater(s, v):
    c3 = v[7] > s
    c2 = jnp.where(c3, v[11], v[3]) > s
    c1 = jnp.where(c3, jnp.where(c2, v[13], v[9]), jnp.where(c2, v[5], v[1])) > s
    t0 = jnp.where(c3,
                   jnp.where(c2, jnp.where(c1, v[14], v[12]), jnp.where(c1, v[10], v[8])),
                   jnp.where(c2, jnp.where(c1, v[6], v[4]), jnp.where(c1, v[2], v[0])))
    c0 = t0 > s
    count = (jnp.where(c3, 8.0, 0.0) + jnp.where(c2, 4.0, 0.0)
             + jnp.where(c1, 2.0, 0.0) + jnp.where(c0, 1.0, 0.0))
    return jnp.where(v[15] > s, float(PEER_TOPK), count)


def _swap_sublanes_with_list(tiles):
    rows = lax.broadcasted_iota(jnp.int32, tiles[0].shape, 0)
    t = list(tiles)
    for d in (4, 2, 1):
        keep = (rows & d) == 0
        for i in range(8):
            if i & d == 0:
                x, y = t[i], t[i + d]
                t[i] = jnp.where(keep, x, pltpu.roll(y, d, 0))
                t[i + d] = jnp.where(keep, pltpu.roll(x, 8 - d, 0), y)
    return t


def _route_kernel(hf_ref, wq_ref, sk_ref, rank1_ref, e1_ref, nk_ref, e0_ref,
                  s_scr, rank_scr, val_scr):
    tmr = hf_ref.shape[0]
    q = jnp.dot(hf_ref[...], wq_ref[...], preferred_element_type=F32).astype(BF16)
    scores = [lax.dot_general(sk_ref[hp], q[:, hp * N_KEYS:(hp + 1) * N_KEYS],
                              (((1,), (1,)), ((), ())), preferred_element_type=F32)
              for hp in range(2 * PEER_HEADS)]
    for p in range(2):
        for kb in range(N_KEYS // 8):
            per_key = _swap_sublanes_with_list(
                [scores[2 * h + p][kb * 8:(kb + 1) * 8, :] for h in range(PEER_HEADS)])
            for i in range(8):
                s_scr[p, kb * 8 + i] = per_key[i]

    flawed = jnp.zeros((PEER_HEADS, tmr), F32)
    for p in range(2):
        s = [s_scr[p, k] for k in range(N_KEYS)]
        v = _top16_sorted(s)
        ranked = jnp.zeros((PEER_HEADS, tmr), F32)
        for k in range(N_KEYS):
            rank = _count_greater(s[k], v)
            rank_scr[p, k] = rank
            ranked = ranked + jnp.where(rank < float(PEER_TOPK), 1.0, 0.0)
        flawed = jnp.maximum(flawed, jnp.abs(ranked - float(PEER_TOPK)))
        for r in range(PEER_TOPK):
            val_scr[p, r] = v[r]
            if r:
                flawed = jnp.maximum(flawed, jnp.where(v[r - 1] > v[r], 0.0, 1.0))

    @pl.when(jnp.max(flawed) > 0.0)
    def _():
        for p in range(2):
            for k in range(N_KEYS):
                rank_scr[p, k] = jnp.zeros((PEER_HEADS, tmr), F32)

            def against(kp, carry):
                other = s_scr[p, kp]
                for k in range(N_KEYS):
                    mine = s_scr[p, k]
                    beats = (other > mine) | ((other == mine) & (kp < k))
                    rank_scr[p, k] = rank_scr[p, k] + jnp.where(beats, 1.0, 0.0)
                return carry

            lax.fori_loop(0, N_KEYS, against, 0)
            for r in range(PEER_TOPK):
                v = jnp.zeros((PEER_HEADS, tmr), F32)
                for k in range(N_KEYS):
                    v = jnp.where(rank_scr[p, k] == float(r), s_scr[p, k], v)
                val_scr[p, r] = v

    cells = _young_cells()
    v0 = [val_scr[0, a] for a in range(PEER_TOPK)]
    v1 = [val_scr[1, b] for b in range(PEER_TOPK)]
    csum = {c: v0[c[0]] + v1[c[1]] for c in cells}
    beaten = {c: jnp.full(csum[c].shape, float((c[0] + 1) * (c[1] + 1) - 1), F32) for c in cells}
    for ix, cx in enumerate(cells):
        for cy in cells[ix + 1:]:
            comparable = (cx[0] <= cy[0] and cx[1] <= cy[1]) or (cy[0] <= cx[0] and cy[1] <= cx[1])
            if comparable:
                continue
            y_wins = jnp.where(csum[cy] > csum[cx], 1.0, 0.0)
            beaten[cx] = beaten[cx] + y_wins
            beaten[cy] = beaten[cy] + (1.0 - y_wins)
    top = csum[(0, 0)]
    zsum = jnp.zeros_like(top)
    ncol = [jnp.zeros_like(top) for _ in range(PEER_TOPK)]
    for c in cells:
        chosen = beaten[c] < float(PEER_TOPK)
        zsum = zsum + jnp.where(chosen, jnp.exp(csum[c] - top), 0.0)
        ncol[c[0]] = ncol[c[0]] + jnp.where(chosen, 1.0, 0.0)
    inv_z = 1.0 / zsum

    for k in range(N_KEYS):
        rank0 = rank_scr[0, k]
        nk = jnp.zeros((PEER_HEADS, tmr), F32)
        for a in range(PEER_TOPK):
            nk = jnp.where(rank0 == float(a), ncol[a], nk)
        nk_ref[k] = nk
        e0_ref[k] = jnp.exp(s_scr[0, k] - v0[0]) * inv_z

    for kb in range(N_KEYS // PACK):
        rank_rows, e1_rows = [], []
        for half in range(PACK // 8):
            keys = range(kb * PACK + half * 8, kb * PACK + half * 8 + 8)
            rank_rows.append(_swap_sublanes_with_list([rank_scr[1, k] for k in keys]))
            e1_rows.append(_swap_sublanes_with_list([jnp.exp(s_scr[1, k] - v1[0]) for k in keys]))
        for h in range(PEER_HEADS):
            rows = slice(kb * PACK, (kb + 1) * PACK)
            rank1_ref[h, rows, :] = jnp.concatenate([part[h] for part in rank_rows], axis=0).astype(BF16)
            e1_ref[h, rows, :] = jnp.concatenate([part[h] for part in e1_rows], axis=0).astype(BF16)


def _route(hf, wq_bf, sk_bf, tmr):
    t, d = hf.shape
    sk2 = sk_bf.reshape(2 * PEER_HEADS, N_KEYS, sk_bf.shape[-1])
    dense = jax.ShapeDtypeStruct((PEER_HEADS, N_KEYS, t), BF16)
    dense_spec = pl.BlockSpec((PEER_HEADS, N_KEYS, tmr), lambda i: (0, 0, i))
    rows = jax.ShapeDtypeStruct((N_KEYS, PEER_HEADS, t), F32)
    rows_spec = pl.BlockSpec((N_KEYS, PEER_HEADS, tmr), lambda i: (0, 0, i))
    return pl.pallas_call(
        _route_kernel,
        grid=(t // tmr,),
        in_specs=[pl.BlockSpec((tmr, d), lambda i: (i, 0)),
                  pl.BlockSpec(wq_bf.shape, lambda i: (0, 0)),
                  pl.BlockSpec(sk2.shape, lambda i: (0, 0, 0))],
        out_specs=[dense_spec, dense_spec, rows_spec, rows_spec],
        out_shape=[dense, dense, rows, rows],
        scratch_shapes=[pltpu.VMEM((2, N_KEYS, PEER_HEADS, tmr), F32),
                        pltpu.VMEM((2, N_KEYS, PEER_HEADS, tmr), F32),
                        pltpu.VMEM((2, PEER_TOPK, PEER_HEADS, tmr), F32)],
        compiler_params=_cparams(("parallel",)),
        name="route",
    )(hf, wq_bf, sk2)


PACK = 16


GATE_LANES = 256


def _gate_rows(row_ref, h, r, lanes):
    row = row_ref[r, h:h + 1, lanes]
    half = jnp.broadcast_to(row, (PACK // 2, row.shape[1]))
    return jnp.concatenate([half, half], axis=0).astype(BF16)


def _peer_gate_chunk(row0, s_ref, w_ref, rank1_ref, e1_ref, nk_ref, e0_ref):
    ec, tm = s_ref.shape
    subs = N_KEYS // PACK
    for g in range(tm // GATE_LANES):
        lanes = slice(g * GATE_LANES, (g + 1) * GATE_LANES)
        for j in range(ec // N_KEYS):
            gate = [jnp.zeros((PACK, GATE_LANES), BF16) for _ in range(subs)]
            for h in range(PEER_HEADS):
                n_rows = _gate_rows(nk_ref, h, row0 + j, lanes)
                e0_rows = _gate_rows(e0_ref, h, row0 + j, lanes)
                for k in range(subs):
                    rows = slice(k * PACK, (k + 1) * PACK)
                    chosen = rank1_ref[h, rows, lanes] < n_rows
                    picked = jnp.where(chosen, e1_ref[h, rows, lanes], jnp.zeros((), BF16))
                    gate[k] = gate[k] + picked * e0_rows
            for k in range(subs):
                rows = slice(j * N_KEYS + k * PACK, j * N_KEYS + (k + 1) * PACK)
                s = s_ref[rows, lanes].astype(BF16)
                act = 0.5 * s * (1.0 + lax.erf(s * (0.5 ** 0.5)))
                w_ref[rows, lanes] = act * gate[k]


def _peer_kernel(u_ref, hft_ref, vt_ref, rank1_ref, e1_ref, nk_ref, e0_ref, yt_ref, s_scr, w_scr):
    c = pl.program_id(1)

    @pl.when(c == 0)
    def _():
        yt_ref[...] = jnp.zeros_like(yt_ref)

    s_scr[...] = jnp.dot(u_ref[...], hft_ref[...], preferred_element_type=F32)
    _peer_gate_chunk(0, s_scr, w_scr, rank1_ref, e1_ref, nk_ref, e0_ref)
    yt_ref[...] += jnp.dot(vt_ref[...], w_scr[...], preferred_element_type=F32)


def _peer(u_bf, hft, vt_bf, rank1, e1, nk, e0, tm, ec):
    n_exp, d = u_bf.shape
    t = hft.shape[1]
    rspec = pl.BlockSpec((PEER_HEADS, N_KEYS, tm), lambda i, c: (0, 0, i))
    rows = pl.BlockSpec((ec // N_KEYS, PEER_HEADS, tm), lambda i, c: (c, 0, i))
    return pl.pallas_call(
        _peer_kernel,
        grid=(t // tm, n_exp // ec),
        in_specs=[pl.BlockSpec((ec, d), lambda i, c: (c, 0)),
                  pl.BlockSpec((d, tm), lambda i, c: (0, i)),
                  pl.BlockSpec((d, ec), lambda i, c: (0, c)),
                  rspec, rspec, rows, rows],
        out_specs=pl.BlockSpec((d, tm), lambda i, c: (0, i)),
        out_shape=jax.ShapeDtypeStruct((d, t), F32),
        scratch_shapes=[pltpu.VMEM((ec, tm), F32), pltpu.VMEM((ec, tm), BF16)],
        compiler_params=_cparams(("parallel", "arbitrary")),
        name="peer",
    )(u_bf, hft, vt_bf, rank1, e1, nk, e0)


def _final_kernel(x1_ref, yt_ref, g_ref, o_ref):
    o_ref[...] = _rms(x1_ref[...] + yt_ref[...].T, g_ref[...])


def _add_peer_kernel(x1_ref, yt_ref, o_ref):
    o_ref[...] = x1_ref[...] + yt_ref[...].T


def _add_peer(x1, yt, tm):
    t, d = x1.shape
    return pl.pallas_call(
        _add_peer_kernel,
        grid=(t // tm,),
        in_specs=[pl.BlockSpec((tm, d), lambda i: (i, 0)), pl.BlockSpec((d, tm), lambda i: (0, i))],
        out_specs=pl.BlockSpec((tm, d), lambda i: (i, 0)),
        out_shape=jax.ShapeDtypeStruct((t, d), F32),
        compiler_params=_cparams(("parallel",)),
        name="add_peer",
    )(x1, yt)


def _final(x1, yt, g, tm):
    t, d = x1.shape
    return pl.pallas_call(
        _final_kernel,
        grid=(t // tm,),
        in_specs=[pl.BlockSpec((tm, d), lambda i: (i, 0)), pl.BlockSpec((d, tm), lambda i: (0, i)),
                  pl.BlockSpec((1, d), lambda i: (0, 0))],
        out_specs=pl.BlockSpec((tm, d), lambda i: (i, 0)),
        out_shape=jax.ShapeDtypeStruct((t, d), F32),
        compiler_params=_cparams(("parallel",)),
        name="final",
    )(x1, yt, g)


def _pick(total, want):
    tile = min(total, want)
    assert total % tile == 0, (total, tile)
    return tile


def kernel(x, norm_mix, w_in, w_pool, pool_scale, w_out, norm_ffn, w_query, sub_keys,
           expert_u, expert_v, norm_final):
    b, s, d = x.shape
    t = b * s
    depth = norm_mix.shape[0]
    d_pool = w_pool.shape[1] * w_pool.shape[2]
    d_qkv = w_in.shape[2] - d_pool
    d_attn = d_qkv // 3
    hp = d_attn // LANES
    assert d_attn == w_out.shape[1] - d_pool and d_attn % LANES == 0
    assert sub_keys.shape[1:] == (PEER_HEADS, 2, N_KEYS, N_KEYS)
    assert expert_u.shape[1] == N_KEYS * N_KEYS
    assert s % (DILATED_PATTERNS[-1][1] * WBLK) == 0

    tm = _pick(s, 512)
    tm_route = _pick(s, 256)
    tm_peer = _pick(s, 1024)
    ec = 1024

    x2 = x.reshape(t, d)
    for layer in range(depth):
        qkv, zp = _in_proj(x2, norm_mix[layer][None], w_in[layer].astype(BF16), d_qkv, tm)
        att = _attention(qkv, b, s)
        x1, hf, hft = _mix_out(att, zp, x2, w_pool[layer].astype(BF16),
                               pool_scale[layer][None], w_out[layer].astype(BF16),
                               norm_ffn[layer][None], s, tm)
        rank1, e1, nk, e0 = _route(hf, w_query[layer].astype(BF16), sub_keys[layer].astype(BF16),
                                   tm_route)
        yt = _peer(expert_u[layer].astype(BF16), hft, expert_v[layer].astype(BF16).T,
                   rank1, e1, nk, e0, tm_peer, ec)
        if layer + 1 < depth:
            x2 = _add_peer(x1, yt, tm)
    out = _final(x1, yt, norm_final[None], tm)
    return out.reshape(b, s, d)
```

```python
import functools

import jax
import jax.numpy as jnp
from jax import lax
from jax.experimental import pallas as pl
from jax.experimental.pallas import tpu as pltpu

F32 = jnp.float32
BF16 = jnp.bfloat16

EPS = 1e-6
NEG = -1e30
HEAD_DIM = 64
LANES = 128
DILATED_PATTERNS = ((128, 1), (512, 4), (2048, 16))
WBLK = 128
POOL_WINDOWS = (2, 4, 8, 16)
POOL_HALO = 16
PEER_HEADS = 8
N_KEYS = 128
PEER_TOPK = 16
VMEM_LIMIT = 56 * 1024 * 1024


def _cparams(sem):
    return pltpu.CompilerParams(dimension_semantics=sem, vmem_limit_bytes=VMEM_LIMIT)


def _rms(x, g):
    ms = jnp.mean(x * x, axis=-1, keepdims=True)
    return x * lax.rsqrt(ms + EPS) * g


def _inproj_kernel(x_ref, g_ref, w_ref, qkv_ref, zp_ref):
    h = _rms(x_ref[...], g_ref[...]).astype(BF16)
    z = jnp.dot(h, w_ref[...], preferred_element_type=F32)
    n_blocks = qkv_ref.shape[0]
    for j in range(n_blocks):
        qkv_ref[j] = z[:, j * LANES:(j + 1) * LANES]
    zp_ref[...] = z[:, n_blocks * LANES:]


def _in_proj(x2, g, w_bf, d_qkv, tm):
    t, d = x2.shape
    e = w_bf.shape[1]
    n_blocks = d_qkv // LANES
    return pl.pallas_call(
        _inproj_kernel,
        grid=(t // tm,),
        in_specs=[pl.BlockSpec((tm, d), lambda i: (i, 0)),
                  pl.BlockSpec((1, d), lambda i: (0, 0)),
                  pl.BlockSpec((d, e), lambda i: (0, 0))],
        out_specs=[pl.BlockSpec((n_blocks, tm, LANES), lambda i: (0, i, 0)),
                   pl.BlockSpec((tm, e - d_qkv), lambda i: (i, 0))],
        out_shape=[jax.ShapeDtypeStruct((n_blocks, t, LANES), F32),
                   jax.ShapeDtypeStruct((t, e - d_qkv), F32)],
        compiler_params=_cparams(("parallel",)),
        name="in_proj",
    )(x2, g, w_bf)


ATTN_TILE = DILATED_PATTERNS[-1][1] * WBLK


def _rows(ref, start, size, stride):
    if stride == 1:
        return ref[start:start + size, :]
    return ref[pl.ds(start, size, stride=stride), :]


def _attn_block(q, kk, vv1, mask2):
    lane_a = lax.broadcasted_iota(jnp.int32, q.shape, 1) < HEAD_DIM
    zero = jnp.zeros((), q.dtype)
    q2 = jnp.concatenate([jnp.where(lane_a, q, zero), jnp.where(lane_a, zero, q)], axis=0)
    s = lax.dot_general(q2, kk, (((1,), (1,)), ((), ())), preferred_element_type=F32)
    s = jnp.where(mask2, s * (HEAD_DIM ** -0.5), NEG)
    m = jnp.max(s, axis=-1, keepdims=True)
    p = jnp.exp(s - m).astype(BF16)
    pv = jnp.dot(p, vv1, preferred_element_type=F32)
    w = q.shape[0]
    num = jnp.where(lane_a, pv[:w, :LANES], pv[w:, :LANES])
    den = jnp.where(lane_a, pv[:w, LANES:], pv[w:, LANES:])
    mm = jnp.where(lane_a, m[:w], m[w:])
    return num / den, mm + jnp.log(den)


def _attn_kernel(q_ref, kp_ref, kc_ref, vp_ref, vc_ref, att_ref, o_scr, l_scr):
    n = pl.program_id(2)
    tile = q_ref.shape[0]
    qi = lax.broadcasted_iota(jnp.int32, (2 * WBLK, 2 * WBLK), 0) % WBLK
    kj = lax.broadcasted_iota(jnp.int32, (2 * WBLK, 2 * WBLK), 1)
    dist = qi + WBLK - kj
    for bi, (window, dil) in enumerate(DILATED_PATTERNS):
        band = (dist >= 0) & (dist <= window // dil)
        band_first = band & ((kj >= WBLK) | (n > 0))
        sub = tile // dil
        ones = jnp.ones((WBLK + sub, LANES), BF16)
        for c in range(dil):
            q = _rows(q_ref, c, sub, dil).astype(BF16)
            halo = tile - WBLK * dil + c
            kcat = jnp.concatenate([_rows(kp_ref, halo, WBLK, dil), _rows(kc_ref, c, sub, dil)],
                                   axis=0).astype(BF16)
            vcat = jnp.concatenate([_rows(vp_ref, halo, WBLK, dil), _rows(vc_ref, c, sub, dil)],
                                   axis=0).astype(BF16)
            vcat = jnp.concatenate([vcat, ones], axis=1)
            for j in range(sub // WBLK):
                o, lse = _attn_block(q[j * WBLK:(j + 1) * WBLK], kcat[j * WBLK:(j + 2) * WBLK],
                                     vcat[j * WBLK:(j + 2) * WBLK], band_first if j == 0 else band)
                first_row = c + dil * j * WBLK
                if dil == 1:
                    o_scr[bi, first_row:first_row + WBLK, :] = o
                    l_scr[bi, first_row:first_row + WBLK, :] = lse
                else:
                    o_scr[bi, pl.ds(first_row, WBLK, stride=dil), :] = o
                    l_scr[bi, pl.ds(first_row, WBLK, stride=dil), :] = lse

    l1, l2, l3 = l_scr[0], l_scr[1], l_scr[2]
    lm = jnp.maximum(jnp.maximum(l1, l2), l3)
    e1, e2, e3 = jnp.exp(l1 - lm), jnp.exp(l2 - lm), jnp.exp(l3 - lm)
    att_ref[...] = (e1 * o_scr[0] + e2 * o_scr[1] + e3 * o_scr[2]) / (e1 + e2 + e3)


def _attention(qkv, b, s):
    hp = qkv.shape[0] // 3
    tile = ATTN_TILE
    qkv5 = qkv.reshape(3 * hp, b, s, LANES)
    blk = (None, None, tile, LANES)

    def spec(which, prev):
        if prev:
            return pl.BlockSpec(blk, lambda bi, hi, ni: (which * hp + hi, bi, jnp.maximum(ni - 1, 0), 0))
        return pl.BlockSpec(blk, lambda bi, hi, ni: (which * hp + hi, bi, ni, 0))

    att = pl.pallas_call(
        _attn_kernel,
        grid=(b, hp, s // tile),
        in_specs=[spec(0, False), spec(1, True), spec(1, False), spec(2, True), spec(2, False)],
        out_specs=pl.BlockSpec(blk, lambda bi, hi, ni: (hi, bi, ni, 0)),
        out_shape=jax.ShapeDtypeStruct((hp, b, s, LANES), F32),
        scratch_shapes=[pltpu.VMEM((len(DILATED_PATTERNS), tile, LANES), F32),
                        pltpu.VMEM((len(DILATED_PATTERNS), tile, LANES), F32)],
        compiler_params=_cparams(("parallel", "parallel", "arbitrary")),
        name="attention",
    )(qkv5, qkv5, qkv5, qkv5, qkv5)
    return att.reshape(hp, b * s, LANES)


def _mixout_kernel(att_ref, zp_ref, zh_ref, x_ref, wp_ref, ps_ref, wo_ref, g_ref,
                   x1_ref, hf_ref, hft_ref, *, tiles_per_seq):
    i = pl.program_id(0)
    tm = x_ref.shape[0]
    first = (i % tiles_per_seq) == 0
    att = jnp.concatenate([att_ref[j] for j in range(att_ref.shape[0])], axis=1)

    zc = zp_ref[...]
    halo = jnp.where(first, 0.0, zh_ref[...])
    buf = jnp.concatenate([zc, halo], axis=0)
    sums = {1: buf}
    w = 1
    while w < POOL_WINDOWS[-1]:
        sums[2 * w] = sums[w] + pltpu.roll(sums[w], w, 0)
        w *= 2
    pos = (i % tiles_per_seq) * tm + lax.broadcasted_iota(jnp.int32, (tm, 1), 0) + 1
    cg = zc.shape[1] // len(POOL_WINDOWS)
    mixed = []
    for g, win in enumerate(POOL_WINDOWS):
        cols = slice(g * cg, (g + 1) * cg)
        cnt = jnp.minimum(pos, win).astype(F32)
        pooled = sums[win][:tm, cols] / cnt - zc[:, cols]
        mixed.append(jnp.dot(pooled.astype(BF16), wp_ref[g], preferred_element_type=F32))
    mixed = jnp.concatenate(mixed, axis=1) * ps_ref[...]

    mix = jnp.concatenate([att, mixed], axis=1).astype(BF16)
    x1 = x_ref[...] + jnp.dot(mix, wo_ref[...], preferred_element_type=F32)
    x1_ref[...] = x1
    hf = _rms(x1, g_ref[...])
    hf_ref[...] = hf.astype(BF16)
    hft_ref[...] = hf.T.astype(BF16)


def _mix_out(att, zp, x2, wp_bf, ps, wo_bf, g, seq, tm):
    t, d = x2.shape
    dp = zp.shape[1]
    row = lambda i: (i, 0)
    const2 = lambda i: (0, 0)
    halo_rows = tm // POOL_HALO
    return pl.pallas_call(
        functools.partial(_mixout_kernel, tiles_per_seq=seq // tm),
        grid=(t // tm,),
        in_specs=[
            pl.BlockSpec((att.shape[0], tm, LANES), lambda i: (0, i, 0)),
            pl.BlockSpec((tm, dp), row),
            pl.BlockSpec((POOL_HALO, dp), lambda i: (jnp.maximum(i * halo_rows - 1, 0), 0)),
            pl.BlockSpec((tm, d), row),
            pl.BlockSpec(wp_bf.shape, lambda i: (0, 0, 0)),
            pl.BlockSpec((1, dp), const2),
            pl.BlockSpec(wo_bf.shape, const2),
            pl.BlockSpec((1, d), const2)],
        out_specs=[pl.BlockSpec((tm, d), row), pl.BlockSpec((tm, d), row),
                   pl.BlockSpec((d, tm), lambda i: (0, i))],
        out_shape=[jax.ShapeDtypeStruct((t, d), F32), jax.ShapeDtypeStruct((t, d), BF16),
                   jax.ShapeDtypeStruct((d, t), BF16)],
        compiler_params=_cparams(("parallel",)),
        name="mix_out",
    )(att, zp, zp, x2, wp_bf, ps, wo_bf, g)


def _young_cells():
    return [(a, b) for a in range(PEER_TOPK) for b in range(PEER_TOPK)
            if (a + 1) * (b + 1) <= PEER_TOPK]


def _sort16_pairs():
    n, pairs, p = PEER_TOPK, [], 1
    while p < n:
        k = p
        while k >= 1:
            for j in range(k % p, n - k, 2 * k):
                for i in range(min(k, n - j - k)):
                    if (i + j) // (2 * p) == (i + j + k) // (2 * p):
                        pairs.append((i + j, i + j + k))
            k //= 2
        p *= 2
    return pairs


def _top16_sorted(vals):
    groups = []
    for g in range(0, len(vals), PEER_TOPK):
        x = list(vals[g:g + PEER_TOPK])
        for a, b in _sort16_pairs():
            x[a], x[b] = jnp.maximum(x[a], x[b]), jnp.minimum(x[a], x[b])
        groups.append(x)
    while len(groups) > 1:
        merged = []
        for x, y in zip(groups[0::2], groups[1::2]):
            z = [jnp.maximum(x[i], y[PEER_TOPK - 1 - i]) for i in range(PEER_TOPK)]
            d = PEER_TOPK // 2
            while d >= 1:
                for i in range(PEER_TOPK):
                    if i & d == 0:
                        z[i], z[i + d] = jnp.maximum(z[i], z[i + d]), jnp.minimum(z[i], z[i + d])
                d //= 2
            merged.append(z)
        groups = merged
    return groups[0]


def _count_greater(s, v):
    c3 = v[7] > s
    c2 = jnp.where(c3, v[11], v[3]) > s
    c1 = jnp.where(c3, jnp.where(c2, v[13], v[9]), jnp.where(c2, v[5], v[1])) > s
    t0 = jnp.where(c3,
                   jnp.where(c2, jnp.where(c1, v[14], v[12]), jnp.where(c1, v[10], v[8])),
                   jnp.where(c2, jnp.where(c1, v[6], v[4]), jnp.where(c1, v[2], v[0])))
    c0 = t0 > s
    count = (jnp.where(c3, 8.0, 0.0) + jnp.where(c2, 4.0, 0.0)
             + jnp.where(c1, 2.0, 0.0) + jnp.where(c0, 1.0, 0.0))
    return jnp.where(v[15] > s, float(PEER_TOPK), count)


def _swap_sublanes_with_list(tiles):
    rows = lax.broadcasted_iota(jnp.int32, tiles[0].shape, 0)
    t = list(tiles)
    for d in (4, 2, 1):
        keep = (rows & d) == 0
        for i in range(8):
            if i & d == 0:
                x, y = t[i], t[i + d]
                t[i] = jnp.where(keep, x, pltpu.roll(y, d, 0))
                t[i + d] = jnp.where(keep, pltpu.roll(x, 8 - d, 0), y)
    return t


def _route_kernel(hf_ref, wq_ref, sk_ref, rank1_ref, e1_ref, nk_ref, e0_ref,
                  s_scr, rank_scr, val_scr):
    tmr = hf_ref.shape[0]
    q = jnp.dot(hf_ref[...], wq_ref[...], preferred_element_type=F32).astype(BF16)
    scores = [lax.dot_general(sk_ref[hp], q[:, hp * N_KEYS:(hp + 1) * N_KEYS],
                              (((1,), (1,)), ((), ())), preferred_element_type=F32)
              for hp in range(2 * PEER_HEADS)]
    for p in range(2):
        for kb in range(N_KEYS // 8):
            per_key = _swap_sublanes_with_list(
                [scores[2 * h + p][kb * 8:(kb + 1) * 8, :] for h in range(PEER_HEADS)])
            for i in range(8):
                s_scr[p, kb * 8 + i] = per_key[i]

    flawed = jnp.zeros((PEER_HEADS, tmr), F32)
    for p in range(2):
        s = [s_scr[p, k] for k in range(N_KEYS)]
        v = _top16_sorted(s)
        ranked = jnp.zeros((PEER_HEADS, tmr), F32)
        for k in range(N_KEYS):
            rank = _count_greater(s[k], v)
            rank_scr[p, k] = rank
            ranked = ranked + jnp.where(rank < float(PEER_TOPK), 1.0, 0.0)
        flawed = jnp.maximum(flawed, jnp.abs(ranked - float(PEER_TOPK)))
        for r in range(PEER_TOPK):
            val_scr[p, r] = v[r]
            if r:
                flawed = jnp.maximum(flawed, jnp.where(v[r - 1] > v[r], 0.0, 1.0))

    @pl.when(jnp.max(flawed) > 0.0)
    def _():
        for p in range(2):
            for k in range(N_KEYS):
                rank_scr[p, k] = jnp.zeros((PEER_HEADS, tmr), F32)

            def against(kp, carry):
                other = s_scr[p, kp]
                for k in range(N_KEYS):
                    mine = s_scr[p, k]
                    beats = (other > mine) | ((other == mine) & (kp < k))
                    rank_scr[p, k] = rank_scr[p, k] + jnp.where(beats, 1.0, 0.0)
                return carry

            lax.fori_loop(0, N_KEYS, against, 0)
            for r in range(PEER_TOPK):
                v = jnp.zeros((PEER_HEADS, tmr), F32)
                for k in range(N_KEYS):
                    v = jnp.where(rank_scr[p, k] == float(r), s_scr[p, k], v)
                val_scr[p, r] = v

    cells = _young_cells()
    v0 = [val_scr[0, a] for a in range(PEER_TOPK)]
    v1 = [val_scr[1, b] for b in range(PEER_TOPK)]
    csum = {c: v0[c[0]] + v1[c[1]] for c in cells}
    beaten = {c: jnp.full(csum[c].shape, float((c[0] + 1) * (c[1] + 1) - 1), F32) for c in cells}
    for ix, cx in enumerate(cells):
        for cy in cells[ix + 1:]:
            comparable = (cx[0] <= cy[0] and cx[1] <= cy[1]) or (cy[0] <= cx[0] and cy[1] <= cx[1])
            if comparable:
                continue
            y_wins = jnp.where(csum[cy] > csum[cx], 1.0, 0.0)
            beaten[cx] = beaten[cx] + y_wins
            beaten[cy] = beaten[cy] + (1.0 - y_wins)
    top = csum[(0, 0)]
    zsum = jnp.zeros_like(top)
    ncol = [jnp.zeros_like(top) for _ in range(PEER_TOPK)]
    for c in cells:
        chosen = beaten[c] < float(PEER_TOPK)
        zsum = zsum + jnp.where(chosen, jnp.exp(csum[c] - top), 0.0)
        ncol[c[0]] = ncol[c[0]] + jnp.where(chosen, 1.0, 0.0)
    inv_z = 1.0 / zsum

    for k in range(N_KEYS):
        rank0 = rank_scr[0, k]
        nk = jnp.zeros((PEER_HEADS, tmr), F32)
        for a in range(PEER_TOPK):
            nk = jnp.where(rank0 == float(a), ncol[a], nk)
        nk_ref[k] = nk
        e0_ref[k] = jnp.exp(s_scr[0, k] - v0[0]) * inv_z

    for kb in range(N_KEYS // PACK):
        rank_rows, e1_rows = [], []
        for half in range(PACK // 8):
            keys = range(kb * PACK + half * 8, kb * PACK + half * 8 + 8)
            rank_rows.append(_swap_sublanes_with_list([rank_scr[1, k] for k in keys]))
            e1_rows.append(_swap_sublanes_with_list([jnp.exp(s_scr[1, k] - v1[0]) for k in keys]))
        for h in range(PEER_HEADS):
            rows = slice(kb * PACK, (kb + 1) * PACK)
            rank1_ref[h, rows, :] = jnp.concatenate([part[h] for part in rank_rows], axis=0).astype(BF16)
            e1_ref[h, rows, :] = jnp.concatenate([part[h] for part in e1_rows], axis=0).astype(BF16)


def _route(hf, wq_bf, sk_bf, tmr):
    t, d = hf.shape
    sk2 = sk_bf.reshape(2 * PEER_HEADS, N_KEYS, sk_bf.shape[-1])
    dense = jax.ShapeDtypeStruct((PEER_HEADS, N_KEYS, t), BF16)
    dense_spec = pl.BlockSpec((PEER_HEADS, N_KEYS, tmr), lambda i: (0, 0, i))
    rows = jax.ShapeDtypeStruct((N_KEYS, PEER_HEADS, t), F32)
    rows_spec = pl.BlockSpec((N_KEYS, PEER_HEADS, tmr), lambda i: (0, 0, i))
    return pl.pallas_call(
        _route_kernel,
        grid=(t // tmr,),
        in_specs=[pl.BlockSpec((tmr, d), lambda i: (i, 0)),
                  pl.BlockSpec(wq_bf.shape, lambda i: (0, 0)),
                  pl.BlockSpec(sk2.shape, lambda i: (0, 0, 0))],
        out_specs=[dense_spec, dense_spec, rows_spec, rows_spec],
        out_shape=[dense, dense, rows, rows],
        scratch_shapes=[pltpu.VMEM((2, N_KEYS, PEER_HEADS, tmr), F32),
                        pltpu.VMEM((2, N_KEYS, PEER_HEADS, tmr), F32),
                        pltpu.VMEM((2, PEER_TOPK, PEER_HEADS, tmr), F32)],
        compiler_params=_cparams(("parallel",)),
        name="route",
    )(hf, wq_bf, sk2)


PACK = 16


GATE_LANES = 256


def _gate_rows(row_ref, h, r, lanes):
    row = row_ref[r, h:h + 1, lanes]
    half = jnp.broadcast_to(row, (PACK // 2, row.shape[1]))
    return jnp.concatenate([half, half], axis=0).astype(BF16)


def _peer_gate_chunk(row0, s_ref, w_ref, rank1_ref, e1_ref, nk_ref, e0_ref):
    ec, tm = s_ref.shape
    subs = N_KEYS // PACK
    for j in range(ec // N_KEYS):
        for g in range(tm // GATE_LANES):
            lanes = slice(g * GATE_LANES, (g + 1) * GATE_LANES)
            gate = [jnp.zeros((PACK, GATE_LANES), BF16) for _ in range(subs)]
            for h in range(PEER_HEADS):
                n_rows = _gate_rows(nk_ref, h, row0 + j, lanes)
                e0_rows = _gate_rows(e0_ref, h, row0 + j, lanes)
                for k in range(subs):
                    rows = slice(k * PACK, (k + 1) * PACK)
                    chosen = rank1_ref[h, rows, lanes] < n_rows
                    picked = jnp.where(chosen, e1_ref[h, rows, lanes], jnp.zeros((), BF16))
                    gate[k] = gate[k] + picked * e0_rows
            for k in range(subs):
                rows = slice(j * N_KEYS + k * PACK, j * N_KEYS + (k + 1) * PACK)
                s = s_ref[rows, lanes].astype(BF16)
                act = 0.5 * s * (1.0 + lax.erf(s * (0.5 ** 0.5)))
                w_ref[rows, lanes] = act * gate[k]


def _peer_kernel(u_ref, hft_ref, vt_ref, rank1_ref, e1_ref, nk_ref, e0_ref, x1_ref, g_ref, o_ref,
                 s_scr, w_scr, yt_scr, *, final_norm):
    c = pl.program_id(1)

    @pl.when(c == 0)
    def _():
        yt_scr[...] = jnp.zeros_like(yt_scr)

    s_scr[...] = jnp.dot(u_ref[...], hft_ref[...], preferred_element_type=F32)
    _peer_gate_chunk(0, s_scr, w_scr, rank1_ref, e1_ref, nk_ref, e0_ref)
    yt_scr[...] += jnp.dot(vt_ref[...], w_scr[...], preferred_element_type=F32)

    @pl.when(c == pl.num_programs(1) - 1)
    def _():
        x2 = x1_ref[...] + yt_scr[...].T
        o_ref[...] = _rms(x2, g_ref[...]) if final_norm else x2


def _peer(u_bf, hft, vt_bf, rank1, e1, nk, e0, x1, g, final_norm, tm, ec):
    n_exp, d = u_bf.shape
    t = hft.shape[1]
    rspec = pl.BlockSpec((PEER_HEADS, N_KEYS, tm), lambda i, c: (0, 0, i))
    rows = pl.BlockSpec((ec // N_KEYS, PEER_HEADS, tm), lambda i, c: (c, 0, i))
    return pl.pallas_call(
        functools.partial(_peer_kernel, final_norm=final_norm),
        grid=(t // tm, n_exp // ec),
        in_specs=[pl.BlockSpec((ec, d), lambda i, c: (c, 0)),
                  pl.BlockSpec((d, tm), lambda i, c: (0, i)),
                  pl.BlockSpec((d, ec), lambda i, c: (0, c)),
                  rspec, rspec, rows, rows,
                  pl.BlockSpec((tm, d), lambda i, c: (i, 0)),
                  pl.BlockSpec((1, d), lambda i, c: (0, 0))],
        out_specs=pl.BlockSpec((tm, d), lambda i, c: (i, 0)),
        out_shape=jax.ShapeDtypeStruct((t, d), F32),
        scratch_shapes=[pltpu.VMEM((ec, tm), F32), pltpu.VMEM((ec, tm), BF16),
                        pltpu.VMEM((d, tm), F32)],
        compiler_params=_cparams(("parallel", "arbitrary")),
        name="peer",
    )(u_bf, hft, vt_bf, rank1, e1, nk, e0, x1, g)


def _pick(total, want):
    tile = min(total, want)
    assert total % tile == 0, (total, tile)
    return tile


def kernel(x, norm_mix, w_in, w_pool, pool_scale, w_out, norm_ffn, w_query, sub_keys,
           expert_u, expert_v, norm_final):
    b, s, d = x.shape
    t = b * s
    depth = norm_mix.shape[0]
    d_pool = w_pool.shape[1] * w_pool.shape[2]
    d_qkv = w_in.shape[2] - d_pool
    d_attn = d_qkv // 3
    hp = d_attn // LANES
    assert d_attn == w_out.shape[1] - d_pool and d_attn % LANES == 0
    assert sub_keys.shape[1:] == (PEER_HEADS, 2, N_KEYS, N_KEYS)
    assert expert_u.shape[1] == N_KEYS * N_KEYS
    assert s % (DILATED_PATTERNS[-1][1] * WBLK) == 0

    tm = _pick(s, 512)
    tm_route = _pick(s, 128)
    tm_peer = _pick(s, 1024)
    ec = 1024

    x2 = x.reshape(t, d)
    for layer in range(depth):
        qkv, zp = _in_proj(x2, norm_mix[layer][None], w_in[layer].astype(BF16), d_qkv, tm)
        att = _attention(qkv, b, s)
        x1, hf, hft = _mix_out(att, zp, x2, w_pool[layer].astype(BF16),
                               pool_scale[layer][None], w_out[layer].astype(BF16),
                               norm_ffn[layer][None], s, tm)
        rank1, e1, nk, e0 = _route(hf, w_query[layer].astype(BF16), sub_keys[layer].astype(BF16),
                                   tm_route)
        last = layer + 1 == depth
        x2 = _peer(expert_u[layer].astype(BF16), hft, expert_v[layer].astype(BF16).T,
                   rank1, e1, nk, e0, x1, norm_final[None], last, tm_peer, ec)
    return x2.reshape(b, s, d)
```

```python
import functools

import jax
import jax.numpy as jnp
from jax import lax
from jax.experimental import pallas as pl
from jax.experimental.pallas import tpu as pltpu

F32 = jnp.float32
BF16 = jnp.bfloat16

EPS = 1e-6
NEG = -1e30
HEAD_DIM = 64
LANES = 128
DILATED_PATTERNS = ((128, 1), (512, 4), (2048, 16))
WBLK = 128
POOL_WINDOWS = (2, 4, 8, 16)
POOL_HALO = 16
PEER_HEADS = 8
N_KEYS = 128
PEER_TOPK = 16
VMEM_LIMIT = 56 * 1024 * 1024


def _cparams(sem):
    return pltpu.CompilerParams(dimension_semantics=sem, vmem_limit_bytes=VMEM_LIMIT)


def _rms(x, g):
    ms = jnp.mean(x * x, axis=-1, keepdims=True)
    return x * lax.rsqrt(ms + EPS) * g


def _inproj_kernel(x_ref, g_ref, w_ref, qkv_ref, zp_ref):
    h = _rms(x_ref[...], g_ref[...]).astype(BF16)
    z = jnp.dot(h, w_ref[...], preferred_element_type=F32)
    n_blocks = qkv_ref.shape[0]
    for j in range(n_blocks):
        qkv_ref[j] = z[:, j * LANES:(j + 1) * LANES]
    zp_ref[...] = z[:, n_blocks * LANES:]


def _in_proj(x2, g, w_bf, d_qkv, tm):
    t, d = x2.shape
    e = w_bf.shape[1]
    n_blocks = d_qkv // LANES
    return pl.pallas_call(
        _inproj_kernel,
        grid=(t // tm,),
        in_specs=[pl.BlockSpec((tm, d), lambda i: (i, 0)),
                  pl.BlockSpec((1, d), lambda i: (0, 0)),
                  pl.BlockSpec((d, e), lambda i: (0, 0))],
        out_specs=[pl.BlockSpec((n_blocks, tm, LANES), lambda i: (0, i, 0)),
                   pl.BlockSpec((tm, e - d_qkv), lambda i: (i, 0))],
        out_shape=[jax.ShapeDtypeStruct((n_blocks, t, LANES), F32),
                   jax.ShapeDtypeStruct((t, e - d_qkv), F32)],
        compiler_params=_cparams(("parallel",)),
        name="in_proj",
    )(x2, g, w_bf)


ATTN_TILE = DILATED_PATTERNS[-1][1] * WBLK


def _rows(ref, start, size, stride):
    if stride == 1:
        return ref[start:start + size, :]
    return ref[pl.ds(start, size, stride=stride), :]


def _attn_block(q, kk, vv1, mask2):
    lane_a = lax.broadcasted_iota(jnp.int32, q.shape, 1) < HEAD_DIM
    zero = jnp.zeros((), q.dtype)
    q2 = jnp.concatenate([jnp.where(lane_a, q, zero), jnp.where(lane_a, zero, q)], axis=0)
    s = lax.dot_general(q2, kk, (((1,), (1,)), ((), ())), preferred_element_type=F32)
    s = jnp.where(mask2, s * (HEAD_DIM ** -0.5), NEG)
    m = jnp.max(s, axis=-1, keepdims=True)
    p = jnp.exp(s - m).astype(BF16)
    pv = jnp.dot(p, vv1, preferred_element_type=F32)
    w = q.shape[0]
    num = jnp.where(lane_a, pv[:w, :LANES], pv[w:, :LANES])
    den = jnp.where(lane_a, pv[:w, LANES:], pv[w:, LANES:])
    mm = jnp.where(lane_a, m[:w], m[w:])
    return num / den, mm + jnp.log(den)


def _attn_kernel(q_ref, kp_ref, kc_ref, vp_ref, vc_ref, att_ref, o_scr, l_scr):
    n = pl.program_id(2)
    tile = q_ref.shape[0]
    qi = lax.broadcasted_iota(jnp.int32, (2 * WBLK, 2 * WBLK), 0) % WBLK
    kj = lax.broadcasted_iota(jnp.int32, (2 * WBLK, 2 * WBLK), 1)
    dist = qi + WBLK - kj
    for bi, (window, dil) in enumerate(DILATED_PATTERNS):
        band = (dist >= 0) & (dist <= window // dil)
        band_first = band & ((kj >= WBLK) | (n > 0))
        sub = tile // dil
        ones = jnp.ones((WBLK + sub, LANES), BF16)
        for c in range(dil):
            q = _rows(q_ref, c, sub, dil).astype(BF16)
            halo = tile - WBLK * dil + c
            kcat = jnp.concatenate([_rows(kp_ref, halo, WBLK, dil), _rows(kc_ref, c, sub, dil)],
                                   axis=0).astype(BF16)
            vcat = jnp.concatenate([_rows(vp_ref, halo, WBLK, dil), _rows(vc_ref, c, sub, dil)],
                                   axis=0).astype(BF16)
            vcat = jnp.concatenate([vcat, ones], axis=1)
            for j in range(sub // WBLK):
                o, lse = _attn_block(q[j * WBLK:(j + 1) * WBLK], kcat[j * WBLK:(j + 2) * WBLK],
                                     vcat[j * WBLK:(j + 2) * WBLK], band_first if j == 0 else band)
                first_row = c + dil * j * WBLK
                if dil == 1:
                    o_scr[bi, first_row:first_row + WBLK, :] = o
                    l_scr[bi, first_row:first_row + WBLK, :] = lse
                else:
                    o_scr[bi, pl.ds(first_row, WBLK, stride=dil), :] = o
                    l_scr[bi, pl.ds(first_row, WBLK, stride=dil), :] = lse

    l1, l2, l3 = l_scr[0], l_scr[1], l_scr[2]
    lm = jnp.maximum(jnp.maximum(l1, l2), l3)
    e1, e2, e3 = jnp.exp(l1 - lm), jnp.exp(l2 - lm), jnp.exp(l3 - lm)
    att_ref[...] = (e1 * o_scr[0] + e2 * o_scr[1] + e3 * o_scr[2]) / (e1 + e2 + e3)


def _attention(qkv, b, s):
    hp = qkv.shape[0] // 3
    tile = ATTN_TILE
    qkv5 = qkv.reshape(3 * hp, b, s, LANES)
    blk = (None, None, tile, LANES)

    def spec(which, prev):
        if prev:
            return pl.BlockSpec(blk, lambda bi, hi, ni: (which * hp + hi, bi, jnp.maximum(ni - 1, 0), 0))
        return pl.BlockSpec(blk, lambda bi, hi, ni: (which * hp + hi, bi, ni, 0))

    att = pl.pallas_call(
        _attn_kernel,
        grid=(b, hp, s // tile),
        in_specs=[spec(0, False), spec(1, True), spec(1, False), spec(2, True), spec(2, False)],
        out_specs=pl.BlockSpec(blk, lambda bi, hi, ni: (hi, bi, ni, 0)),
        out_shape=jax.ShapeDtypeStruct((hp, b, s, LANES), F32),
        scratch_shapes=[pltpu.VMEM((len(DILATED_PATTERNS), tile, LANES), F32),
                        pltpu.VMEM((len(DILATED_PATTERNS), tile, LANES), F32)],
        compiler_params=_cparams(("parallel", "parallel", "arbitrary")),
        name="attention",
    )(qkv5, qkv5, qkv5, qkv5, qkv5)
    return att.reshape(hp, b * s, LANES)


def _mixout_kernel(att_ref, zp_ref, zh_ref, x_ref, wp_ref, ps_ref, wo_ref, g_ref,
                   x1_ref, hf_ref, hft_ref, *, tiles_per_seq):
    i = pl.program_id(0)
    tm = x_ref.shape[0]
    first = (i % tiles_per_seq) == 0
    att = jnp.concatenate([att_ref[j] for j in range(att_ref.shape[0])], axis=1)

    zc = zp_ref[...]
    halo = jnp.where(first, 0.0, zh_ref[...])
    buf = jnp.concatenate([zc, halo], axis=0)
    sums = {1: buf}
    w = 1
    while w < POOL_WINDOWS[-1]:
        sums[2 * w] = sums[w] + pltpu.roll(sums[w], w, 0)
        w *= 2
    pos = (i % tiles_per_seq) * tm + lax.broadcasted_iota(jnp.int32, (tm, 1), 0) + 1
    cg = zc.shape[1] // len(POOL_WINDOWS)
    mixed = []
    for g, win in enumerate(POOL_WINDOWS):
        cols = slice(g * cg, (g + 1) * cg)
        cnt = jnp.minimum(pos, win).astype(F32)
        pooled = sums[win][:tm, cols] / cnt - zc[:, cols]
        mixed.append(jnp.dot(pooled.astype(BF16), wp_ref[g], preferred_element_type=F32))
    mixed = jnp.concatenate(mixed, axis=1) * ps_ref[...]

    mix = jnp.concatenate([att, mixed], axis=1).astype(BF16)
    x1 = x_ref[...] + jnp.dot(mix, wo_ref[...], preferred_element_type=F32)
    x1_ref[...] = x1
    hf = _rms(x1, g_ref[...])
    hf_ref[...] = hf.astype(BF16)
    hft_ref[...] = hf.T.astype(BF16)


def _mix_out(att, zp, x2, wp_bf, ps, wo_bf, g, seq, tm):
    t, d = x2.shape
    dp = zp.shape[1]
    row = lambda i: (i, 0)
    const2 = lambda i: (0, 0)
    halo_rows = tm // POOL_HALO
    return pl.pallas_call(
        functools.partial(_mixout_kernel, tiles_per_seq=seq // tm),
        grid=(t // tm,),
        in_specs=[
            pl.BlockSpec((att.shape[0], tm, LANES), lambda i: (0, i, 0)),
            pl.BlockSpec((tm, dp), row),
            pl.BlockSpec((POOL_HALO, dp), lambda i: (jnp.maximum(i * halo_rows - 1, 0), 0)),
            pl.BlockSpec((tm, d), row),
            pl.BlockSpec(wp_bf.shape, lambda i: (0, 0, 0)),
            pl.BlockSpec((1, dp), const2),
            pl.BlockSpec(wo_bf.shape, const2),
            pl.BlockSpec((1, d), const2)],
        out_specs=[pl.BlockSpec((tm, d), row), pl.BlockSpec((tm, d), row),
                   pl.BlockSpec((d, tm), lambda i: (0, i))],
        out_shape=[jax.ShapeDtypeStruct((t, d), F32), jax.ShapeDtypeStruct((t, d), BF16),
                   jax.ShapeDtypeStruct((d, t), BF16)],
        compiler_params=_cparams(("parallel",)),
        name="mix_out",
    )(att, zp, zp, x2, wp_bf, ps, wo_bf, g)


def _young_cells():
    return [(a, b) for a in range(PEER_TOPK) for b in range(PEER_TOPK)
            if (a + 1) * (b + 1) <= PEER_TOPK]


def _sort16_pairs():
    n, pairs, p = PEER_TOPK, [], 1
    while p < n:
        k = p
        while k >= 1:
            for j in range(k % p, n - k, 2 * k):
                for i in range(min(k, n - j - k)):
                    if (i + j) // (2 * p) == (i + j + k) // (2 * p):
                        pairs.append((i + j, i + j + k))
            k //= 2
        p *= 2
    return pairs


def _top16_sorted(vals):
    groups = []
    for g in range(0, len(vals), PEER_TOPK):
        x = list(vals[g:g + PEER_TOPK])
        for a, b in _sort16_pairs():
            x[a], x[b] = jnp.maximum(x[a], x[b]), jnp.minimum(x[a], x[b])
        groups.append(x)
    while len(groups) > 1:
        merged = []
        for x, y in zip(groups[0::2], groups[1::2]):
            z = [jnp.maximum(x[i], y[PEER_TOPK - 1 - i]) for i in range(PEER_TOPK)]
            d = PEER_TOPK // 2
            while d >= 1:
                for i in range(PEER_TOPK):
                    if i & d == 0:
                        z[i], z[i + d] = jnp.maximum(z[i], z[i + d]), jnp.minimum(z[i], z[i + d])
                d //= 2
            merged.append(z)
        groups = merged
    return groups[0]


def _bisect(s, v):
    c3 = v[7] > s
    c2 = jnp.where(c3, v[11], v[3]) > s
    c1 = jnp.where(c3, jnp.where(c2, v[13], v[9]), jnp.where(c2, v[5], v[1])) > s
    t0 = jnp.where(c3,
                   jnp.where(c2, jnp.where(c1, v[14], v[12]), jnp.where(c1, v[10], v[8])),
                   jnp.where(c2, jnp.where(c1, v[6], v[4]), jnp.where(c1, v[2], v[0])))
    return c3, c2, c1, t0 > s


def _count_greater(s, v):
    c3, c2, c1, c0 = _bisect(s, v)
    count = (jnp.where(c3, 8.0, 0.0) + jnp.where(c2, 4.0, 0.0)
             + jnp.where(c1, 2.0, 0.0) + jnp.where(c0, 1.0, 0.0))
    return jnp.where(v[15] > s, float(PEER_TOPK), count)


def _table_at_count(s, v, table):
    c3, c2, c1, c0 = _bisect(s, v)

    def pick(lo, bits):
        if not bits:
            return table[lo]
        span = 1 << (len(bits) - 1)
        return jnp.where(bits[0], pick(lo + span, bits[1:]), pick(lo, bits[1:]))

    return jnp.where(v[15] > s, 0.0, pick(0, (c3, c2, c1, c0)))


def _swap_sublanes_with_list(tiles):
    rows = lax.broadcasted_iota(jnp.int32, tiles[0].shape, 0)
    t = list(tiles)
    for d in (4, 2, 1):
        keep = (rows & d) == 0
        for i in range(8):
            if i & d == 0:
                x, y = t[i], t[i + d]
                t[i] = jnp.where(keep, x, pltpu.roll(y, d, 0))
                t[i + d] = jnp.where(keep, pltpu.roll(x, 8 - d, 0), y)
    return t


def _route_kernel(hf_ref, wq_ref, sk_ref, rank1_ref, e1_ref, nk_ref, e0_ref,
                  s_scr, rank_scr, val_scr):
    tmr = hf_ref.shape[0]
    q = jnp.dot(hf_ref[...], wq_ref[...], preferred_element_type=F32).astype(BF16)
    scores = [lax.dot_general(sk_ref[hp], q[:, hp * N_KEYS:(hp + 1) * N_KEYS],
                              (((1,), (1,)), ((), ())), preferred_element_type=F32)
              for hp in range(2 * PEER_HEADS)]
    for p in range(2):
        for kb in range(N_KEYS // 8):
            per_key = _swap_sublanes_with_list(
                [scores[2 * h + p][kb * 8:(kb + 1) * 8, :] for h in range(PEER_HEADS)])
            for i in range(8):
                s_scr[p, kb * 8 + i] = per_key[i]

    flawed = jnp.zeros((PEER_HEADS, tmr), F32)
    for p in range(2):
        s = [s_scr[p, k] for k in range(N_KEYS)]
        v = _top16_sorted(s)
        reach = jnp.zeros((PEER_HEADS, tmr), F32)
        for k in range(N_KEYS):
            reach = reach + jnp.where(v[PEER_TOPK - 1] > s[k], 0.0, 1.0)
        flawed = jnp.maximum(flawed, jnp.abs(reach - float(PEER_TOPK)))
        for r in range(PEER_TOPK):
            val_scr[p, r] = v[r]
            if r:
                flawed = jnp.maximum(flawed, jnp.where(v[r - 1] > v[r], 0.0, 1.0))
    tied = jnp.max(flawed) > 0.0

    @pl.when(tied)
    def _():
        for p in range(2):
            for k in range(N_KEYS):
                rank_scr[p, k] = jnp.zeros((PEER_HEADS, tmr), F32)

            def against(kp, carry):
                other = s_scr[p, kp]
                for k in range(N_KEYS):
                    mine = s_scr[p, k]
                    beats = (other > mine) | ((other == mine) & (kp < k))
                    rank_scr[p, k] = rank_scr[p, k] + jnp.where(beats, 1.0, 0.0)
                return carry

            lax.fori_loop(0, N_KEYS, against, 0)
            for r in range(PEER_TOPK):
                v = jnp.zeros((PEER_HEADS, tmr), F32)
                for k in range(N_KEYS):
                    v = jnp.where(rank_scr[p, k] == float(r), s_scr[p, k], v)
                val_scr[p, r] = v

    cells = _young_cells()
    v0 = [val_scr[0, a] for a in range(PEER_TOPK)]
    v1 = [val_scr[1, b] for b in range(PEER_TOPK)]
    csum = {c: v0[c[0]] + v1[c[1]] for c in cells}
    beaten = {c: jnp.full(csum[c].shape, float((c[0] + 1) * (c[1] + 1) - 1), F32) for c in cells}
    for ix, cx in enumerate(cells):
        for cy in cells[ix + 1:]:
            comparable = (cx[0] <= cy[0] and cx[1] <= cy[1]) or (cy[0] <= cx[0] and cy[1] <= cx[1])
            if comparable:
                continue
            y_wins = jnp.where(csum[cy] > csum[cx], 1.0, 0.0)
            beaten[cx] = beaten[cx] + y_wins
            beaten[cy] = beaten[cy] + (1.0 - y_wins)
    top = csum[(0, 0)]
    zsum = jnp.zeros_like(top)
    ncol = [jnp.zeros_like(top) for _ in range(PEER_TOPK)]
    for c in cells:
        chosen = beaten[c] < float(PEER_TOPK)
        zsum = zsum + jnp.where(chosen, jnp.exp(csum[c] - top), 0.0)
        ncol[c[0]] = ncol[c[0]] + jnp.where(chosen, 1.0, 0.0)
    inv_z = 1.0 / zsum

    def emit(partners_of_key, rank_of_key):
        for k in range(N_KEYS):
            nk_ref[k] = partners_of_key(k)
            e0_ref[k] = jnp.exp(s_scr[0, k] - v0[0]) * inv_z
        for kb in range(N_KEYS // PACK):
            rank_rows, e1_rows = [], []
            for half in range(PACK // 8):
                keys = range(kb * PACK + half * 8, kb * PACK + half * 8 + 8)
                rank_rows.append(_swap_sublanes_with_list([rank_of_key(k) for k in keys]))
                e1_rows.append(_swap_sublanes_with_list([jnp.exp(s_scr[1, k] - v1[0]) for k in keys]))
            for h in range(PEER_HEADS):
                rows = slice(kb * PACK, (kb + 1) * PACK)
                rank1_ref[h, rows, :] = jnp.concatenate([part[h] for part in rank_rows], axis=0).astype(BF16)
                e1_ref[h, rows, :] = jnp.concatenate([part[h] for part in e1_rows], axis=0).astype(BF16)

    @pl.when(jnp.logical_not(tied))
    def _():
        emit(lambda k: _table_at_count(s_scr[0, k], v0, ncol),
             lambda k: _count_greater(s_scr[1, k], v1))

    @pl.when(tied)
    def _():
        def partners(k):
            rank0 = rank_scr[0, k]
            nk = jnp.zeros((PEER_HEADS, tmr), F32)
            for a in range(PEER_TOPK):
                nk = jnp.where(rank0 == float(a), ncol[a], nk)
            return nk

        emit(partners, lambda k: rank_scr[1, k])


def _route(hf, wq_bf, sk_bf, tmr):
    t, d = hf.shape
    sk2 = sk_bf.reshape(2 * PEER_HEADS, N_KEYS, sk_bf.shape[-1])
    dense = jax.ShapeDtypeStruct((PEER_HEADS, N_KEYS, t), BF16)
    dense_spec = pl.BlockSpec((PEER_HEADS, N_KEYS, tmr), lambda i: (0, 0, i))
    rows = jax.ShapeDtypeStruct((N_KEYS, PEER_HEADS, t), F32)
    rows_spec = pl.BlockSpec((N_KEYS, PEER_HEADS, tmr), lambda i: (0, 0, i))
    return pl.pallas_call(
        _route_kernel,
        grid=(t // tmr,),
        in_specs=[pl.BlockSpec((tmr, d), lambda i: (i, 0)),
                  pl.BlockSpec(wq_bf.shape, lambda i: (0, 0)),
                  pl.BlockSpec(sk2.shape, lambda i: (0, 0, 0))],
        out_specs=[dense_spec, dense_spec, rows_spec, rows_spec],
        out_shape=[dense, dense, rows, rows],
        scratch_shapes=[pltpu.VMEM((2, N_KEYS, PEER_HEADS, tmr), F32),
                        pltpu.VMEM((2, N_KEYS, PEER_HEADS, tmr), F32),
                        pltpu.VMEM((2, PEER_TOPK, PEER_HEADS, tmr), F32)],
        compiler_params=_cparams(("parallel",)),
        name="route",
    )(hf, wq_bf, sk2)


PACK = 16


GATE_LANES = 256


def _gate_rows(row_ref, h, r, lanes):
    row = row_ref[r, h:h + 1, lanes]
    half = jnp.broadcast_to(row, (PACK // 2, row.shape[1]))
    return jnp.concatenate([half, half], axis=0).astype(BF16)


def _peer_gate_chunk(row0, s_ref, w_ref, rank1_ref, e1_ref, nk_ref, e0_ref):
    ec, tm = s_ref.shape
    subs = N_KEYS // PACK
    for j in range(ec // N_KEYS):
        for g in range(tm // GATE_LANES):
            lanes = slice(g * GATE_LANES, (g + 1) * GATE_LANES)
            gate = [jnp.zeros((PACK, GATE_LANES), BF16) for _ in range(subs)]
            for h in range(PEER_HEADS):
                n_rows = _gate_rows(nk_ref, h, row0 + j, lanes)
                e0_rows = _gate_rows(e0_ref, h, row0 + j, lanes)
                for k in range(subs):
                    rows = slice(k * PACK, (k + 1) * PACK)
                    chosen = rank1_ref[h, rows, lanes] < n_rows
                    picked = jnp.where(chosen, e1_ref[h, rows, lanes], jnp.zeros((), BF16))
                    gate[k] = gate[k] + picked * e0_rows
            for k in range(subs):
                rows = slice(j * N_KEYS + k * PACK, j * N_KEYS + (k + 1) * PACK)
                s = s_ref[rows, lanes].astype(BF16)
                act = 0.5 * s * (1.0 + lax.erf(s * (0.5 ** 0.5)))
                w_ref[rows, lanes] = act * gate[k]


def _peer_kernel(u_ref, hft_ref, vt_ref, rank1_ref, e1_ref, nk_ref, e0_ref, x1_ref, g_ref, o_ref,
                 s_scr, w_scr, yt_scr, *, final_norm):
    c = pl.program_id(1)

    @pl.when(c == 0)
    def _():
        yt_scr[...] = jnp.zeros_like(yt_scr)

    s_scr[...] = jnp.dot(u_ref[...], hft_ref[...], preferred_element_type=F32)
    _peer_gate_chunk(0, s_scr, w_scr, rank1_ref, e1_ref, nk_ref, e0_ref)
    yt_scr[...] += jnp.dot(vt_ref[...], w_scr[...], preferred_element_type=F32)

    @pl.when(c == pl.num_programs(1) - 1)
    def _():
        x2 = x1_ref[...] + yt_scr[...].T
        o_ref[...] = _rms(x2, g_ref[...]) if final_norm else x2


def _peer(u_bf, hft, vt_bf, rank1, e1, nk, e0, x1, g, final_norm, tm, ec):
    n_exp, d = u_bf.shape
    t = hft.shape[1]
    rspec = pl.BlockSpec((PEER_HEADS, N_KEYS, tm), lambda i, c: (0, 0, i))
    rows = pl.BlockSpec((ec // N_KEYS, PEER_HEADS, tm), lambda i, c: (c, 0, i))
    return pl.pallas_call(
        functools.partial(_peer_kernel, final_norm=final_norm),
        grid=(t // tm, n_exp // ec),
        in_specs=[pl.BlockSpec((ec, d), lambda i, c: (c, 0)),
                  pl.BlockSpec((d, tm), lambda i, c: (0, i)),
                  pl.BlockSpec((d, ec), lambda i, c: (0, c)),
                  rspec, rspec, rows, rows,
                  pl.BlockSpec((tm, d), lambda i, c: (i, 0)),
                  pl.BlockSpec((1, d), lambda i, c: (0, 0))],
        out_specs=pl.BlockSpec((tm, d), lambda i, c: (i, 0)),
        out_shape=jax.ShapeDtypeStruct((t, d), F32),
        scratch_shapes=[pltpu.VMEM((ec, tm), F32), pltpu.VMEM((ec, tm), BF16),
                        pltpu.VMEM((d, tm), F32)],
        compiler_params=_cparams(("parallel", "arbitrary")),
        name="peer",
    )(u_bf, hft, vt_bf, rank1, e1, nk, e0, x1, g)


def _pick(total, want):
    tile = min(total, want)
    assert total % tile == 0, (total, tile)
    return tile


def kernel(x, norm_mix, w_in, w_pool, pool_scale, w_out, norm_ffn, w_query, sub_keys,
           expert_u, expert_v, norm_final):
    b, s, d = x.shape
    t = b * s
    depth = norm_mix.shape[0]
    d_pool = w_pool.shape[1] * w_pool.shape[2]
    d_qkv = w_in.shape[2] - d_pool
    d_attn = d_qkv // 3
    hp = d_attn // LANES
    assert d_attn == w_out.shape[1] - d_pool and d_attn % LANES == 0
    assert sub_keys.shape[1:] == (PEER_HEADS, 2, N_KEYS, N_KEYS)
    assert expert_u.shape[1] == N_KEYS * N_KEYS
    assert s % (DILATED_PATTERNS[-1][1] * WBLK) == 0

    tm = _pick(s, 512)
    tm_route = _pick(s, 128)
    tm_peer = _pick(s, 1024)
    ec = 1024

    x2 = x.reshape(t, d)
    for layer in range(depth):
        qkv, zp = _in_proj(x2, norm_mix[layer][None], w_in[layer].astype(BF16), d_qkv, tm)
        att = _attention(qkv, b, s)
        x1, hf, hft = _mix_out(att, zp, x2, w_pool[layer].astype(BF16),
                               pool_scale[layer][None], w_out[layer].astype(BF16),
                               norm_ffn[layer][None], s, tm)
        rank1, e1, nk, e0 = _route(hf, w_query[layer].astype(BF16), sub_keys[layer].astype(BF16),
                                   tm_route)
        last = layer + 1 == depth
        x2 = _peer(expert_u[layer].astype(BF16), hft, expert_v[layer].astype(BF16).T,
                   rank1, e1, nk, e0, x1, norm_final[None], last, tm_peer, ec)
    return x2.reshape(b, s, d)
```

```python
import functools

import jax
import jax.numpy as jnp
from jax import lax
from jax.experimental import pallas as pl
from jax.experimental.pallas import tpu as pltpu

F32 = jnp.float32
BF16 = jnp.bfloat16

EPS = 1e-6
NEG = -1e30
HEAD_DIM = 64
LANES = 128
DILATED_PATTERNS = ((128, 1), (512, 4), (2048, 16))
WBLK = 128
POOL_WINDOWS = (2, 4, 8, 16)
POOL_HALO = 16
PEER_HEADS = 8
N_KEYS = 128
PEER_TOPK = 16
VMEM_LIMIT = 56 * 1024 * 1024


def _cparams(sem):
    return pltpu.CompilerParams(dimension_semantics=sem, vmem_limit_bytes=VMEM_LIMIT)


def _rms(x, g):
    ms = jnp.mean(x * x, axis=-1, keepdims=True)
    return x * lax.rsqrt(ms + EPS) * g


def _inproj_kernel(x_ref, g_ref, w_ref, qkv_ref, zp_ref):
    h = _rms(x_ref[...], g_ref[...]).astype(BF16)
    z = jnp.dot(h, w_ref[...], preferred_element_type=F32)
    n_blocks = qkv_ref.shape[0]
    for j in range(n_blocks):
        qkv_ref[j] = z[:, j * LANES:(j + 1) * LANES]
    zp_ref[...] = z[:, n_blocks * LANES:]


def _in_proj(x2, g, w_bf, d_qkv, tm):
    t, d = x2.shape
    e = w_bf.shape[1]
    n_blocks = d_qkv // LANES
    return pl.pallas_call(
        _inproj_kernel,
        grid=(t // tm,),
        in_specs=[pl.BlockSpec((tm, d), lambda i: (i, 0)),
                  pl.BlockSpec((1, d), lambda i: (0, 0)),
                  pl.BlockSpec((d, e), lambda i: (0, 0))],
        out_specs=[pl.BlockSpec((n_blocks, tm, LANES), lambda i: (0, i, 0)),
                   pl.BlockSpec((tm, e - d_qkv), lambda i: (i, 0))],
        out_shape=[jax.ShapeDtypeStruct((n_blocks, t, LANES), F32),
                   jax.ShapeDtypeStruct((t, e - d_qkv), F32)],
        compiler_params=_cparams(("parallel",)),
        name="in_proj",
    )(x2, g, w_bf)


ATTN_TILE = DILATED_PATTERNS[-1][1] * WBLK


def _rows(ref, start, size, stride):
    if stride == 1:
        return ref[start:start + size, :]
    return ref[pl.ds(start, size, stride=stride), :]


def _attn_block(q, kk, vv1, mask2):
    lane_a = lax.broadcasted_iota(jnp.int32, q.shape, 1) < HEAD_DIM
    zero = jnp.zeros((), q.dtype)
    q2 = jnp.concatenate([jnp.where(lane_a, q, zero), jnp.where(lane_a, zero, q)], axis=0)
    s = lax.dot_general(q2, kk, (((1,), (1,)), ((), ())), preferred_element_type=F32)
    s = jnp.where(mask2, s * (HEAD_DIM ** -0.5), NEG)
    m = jnp.max(s, axis=-1, keepdims=True)
    p = jnp.exp(s - m).astype(BF16)
    pv = jnp.dot(p, vv1, preferred_element_type=F32)
    w = q.shape[0]
    num = jnp.where(lane_a, pv[:w, :LANES], pv[w:, :LANES])
    den = jnp.where(lane_a, pv[:w, LANES:], pv[w:, LANES:])
    mm = jnp.where(lane_a, m[:w], m[w:])
    return num / den, mm + jnp.log(den)


def _attn_kernel(q_ref, kp_ref, kc_ref, vp_ref, vc_ref, att_ref, o_scr, l_scr):
    n = pl.program_id(2)
    tile = q_ref.shape[0]
    qi = lax.broadcasted_iota(jnp.int32, (2 * WBLK, 2 * WBLK), 0) % WBLK
    kj = lax.broadcasted_iota(jnp.int32, (2 * WBLK, 2 * WBLK), 1)
    dist = qi + WBLK - kj
    for bi, (window, dil) in enumerate(DILATED_PATTERNS):
        band = (dist >= 0) & (dist <= window // dil)
        band_first = band & ((kj >= WBLK) | (n > 0))
        sub = tile // dil
        ones = jnp.ones((WBLK + sub, LANES), BF16)
        for c in range(dil):
            q = _rows(q_ref, c, sub, dil).astype(BF16)
            halo = tile - WBLK * dil + c
            kcat = jnp.concatenate([_rows(kp_ref, halo, WBLK, dil), _rows(kc_ref, c, sub, dil)],
                                   axis=0).astype(BF16)
            vcat = jnp.concatenate([_rows(vp_ref, halo, WBLK, dil), _rows(vc_ref, c, sub, dil)],
                                   axis=0).astype(BF16)
            vcat = jnp.concatenate([vcat, ones], axis=1)
            for j in range(sub // WBLK):
                o, lse = _attn_block(q[j * WBLK:(j + 1) * WBLK], kcat[j * WBLK:(j + 2) * WBLK],
                                     vcat[j * WBLK:(j + 2) * WBLK], band_first if j == 0 else band)
                first_row = c + dil * j * WBLK
                if dil == 1:
                    o_scr[bi, first_row:first_row + WBLK, :] = o
                    l_scr[bi, first_row:first_row + WBLK, :] = lse
                else:
                    o_scr[bi, pl.ds(first_row, WBLK, stride=dil), :] = o
                    l_scr[bi, pl.ds(first_row, WBLK, stride=dil), :] = lse

    l1, l2, l3 = l_scr[0], l_scr[1], l_scr[2]
    lm = jnp.maximum(jnp.maximum(l1, l2), l3)
    e1, e2, e3 = jnp.exp(l1 - lm), jnp.exp(l2 - lm), jnp.exp(l3 - lm)
    att_ref[...] = (e1 * o_scr[0] + e2 * o_scr[1] + e3 * o_scr[2]) / (e1 + e2 + e3)


def _attention(qkv, b, s):
    hp = qkv.shape[0] // 3
    tile = ATTN_TILE
    qkv5 = qkv.reshape(3 * hp, b, s, LANES)
    blk = (None, None, tile, LANES)

    def spec(which, prev):
        if prev:
            return pl.BlockSpec(blk, lambda bi, hi, ni: (which * hp + hi, bi, jnp.maximum(ni - 1, 0), 0))
        return pl.BlockSpec(blk, lambda bi, hi, ni: (which * hp + hi, bi, ni, 0))

    att = pl.pallas_call(
        _attn_kernel,
        grid=(b, hp, s // tile),
        in_specs=[spec(0, False), spec(1, True), spec(1, False), spec(2, True), spec(2, False)],
        out_specs=pl.BlockSpec(blk, lambda bi, hi, ni: (hi, bi, ni, 0)),
        out_shape=jax.ShapeDtypeStruct((hp, b, s, LANES), F32),
        scratch_shapes=[pltpu.VMEM((len(DILATED_PATTERNS), tile, LANES), F32),
                        pltpu.VMEM((len(DILATED_PATTERNS), tile, LANES), F32)],
        compiler_params=_cparams(("parallel", "parallel", "arbitrary")),
        name="attention",
    )(qkv5, qkv5, qkv5, qkv5, qkv5)
    return att.reshape(hp, b * s, LANES)


def _mixout_kernel(att_ref, zp_ref, zh_ref, x_ref, wp_ref, ps_ref, wo_ref, g_ref,
                   x1_ref, hf_ref, hft_ref, *, tiles_per_seq):
    i = pl.program_id(0)
    tm = x_ref.shape[0]
    first = (i % tiles_per_seq) == 0
    att = jnp.concatenate([att_ref[j] for j in range(att_ref.shape[0])], axis=1)

    zc = zp_ref[...]
    halo = jnp.where(first, 0.0, zh_ref[...])
    buf = jnp.concatenate([zc, halo], axis=0)
    sums = {1: buf}
    w = 1
    while w < POOL_WINDOWS[-1]:
        sums[2 * w] = sums[w] + pltpu.roll(sums[w], w, 0)
        w *= 2
    pos = (i % tiles_per_seq) * tm + lax.broadcasted_iota(jnp.int32, (tm, 1), 0) + 1
    cg = zc.shape[1] // len(POOL_WINDOWS)
    mixed = []
    for g, win in enumerate(POOL_WINDOWS):
        cols = slice(g * cg, (g + 1) * cg)
        cnt = jnp.minimum(pos, win).astype(F32)
        pooled = sums[win][:tm, cols] / cnt - zc[:, cols]
        mixed.append(jnp.dot(pooled.astype(BF16), wp_ref[g], preferred_element_type=F32))
    mixed = jnp.concatenate(mixed, axis=1) * ps_ref[...]

    mix = jnp.concatenate([att, mixed], axis=1).astype(BF16)
    x1 = x_ref[...] + jnp.dot(mix, wo_ref[...], preferred_element_type=F32)
    x1_ref[...] = x1
    hf = _rms(x1, g_ref[...])
    hf_ref[...] = hf.astype(BF16)
    hft_ref[...] = hf.T.astype(BF16)


def _mix_out(att, zp, x2, wp_bf, ps, wo_bf, g, seq, tm):
    t, d = x2.shape
    dp = zp.shape[1]
    row = lambda i: (i, 0)
    const2 = lambda i: (0, 0)
    halo_rows = tm // POOL_HALO
    return pl.pallas_call(
        functools.partial(_mixout_kernel, tiles_per_seq=seq // tm),
        grid=(t // tm,),
        in_specs=[
            pl.BlockSpec((att.shape[0], tm, LANES), lambda i: (0, i, 0)),
            pl.BlockSpec((tm, dp), row),
            pl.BlockSpec((POOL_HALO, dp), lambda i: (jnp.maximum(i * halo_rows - 1, 0), 0)),
            pl.BlockSpec((tm, d), row),
            pl.BlockSpec(wp_bf.shape, lambda i: (0, 0, 0)),
            pl.BlockSpec((1, dp), const2),
            pl.BlockSpec(wo_bf.shape, const2),
            pl.BlockSpec((1, d), const2)],
        out_specs=[pl.BlockSpec((tm, d), row), pl.BlockSpec((tm, d), row),
                   pl.BlockSpec((d, tm), lambda i: (0, i))],
        out_shape=[jax.ShapeDtypeStruct((t, d), F32), jax.ShapeDtypeStruct((t, d), BF16),
                   jax.ShapeDtypeStruct((d, t), BF16)],
        compiler_params=_cparams(("parallel",)),
        name="mix_out",
    )(att, zp, zp, x2, wp_bf, ps, wo_bf, g)


def _young_cells():
    return [(a, b) for a in range(PEER_TOPK) for b in range(PEER_TOPK)
            if (a + 1) * (b + 1) <= PEER_TOPK]


def _sort16_pairs():
    n, pairs, p = PEER_TOPK, [], 1
    while p < n:
        k = p
        while k >= 1:
            for j in range(k % p, n - k, 2 * k):
                for i in range(min(k, n - j - k)):
                    if (i + j) // (2 * p) == (i + j + k) // (2 * p):
                        pairs.append((i + j, i + j + k))
            k //= 2
        p *= 2
    return pairs


def _top16_sorted(vals):
    groups = []
    for g in range(0, len(vals), PEER_TOPK):
        x = list(vals[g:g + PEER_TOPK])
        for a, b in _sort16_pairs():
            x[a], x[b] = jnp.maximum(x[a], x[b]), jnp.minimum(x[a], x[b])
        groups.append(x)
    while len(groups) > 1:
        merged = []
        for x, y in zip(groups[0::2], groups[1::2]):
            z = [jnp.maximum(x[i], y[PEER_TOPK - 1 - i]) for i in range(PEER_TOPK)]
            d = PEER_TOPK // 2
            while d >= 1:
                for i in range(PEER_TOPK):
                    if i & d == 0:
                        z[i], z[i + d] = jnp.maximum(z[i], z[i + d]), jnp.minimum(z[i], z[i + d])
                d //= 2
            merged.append(z)
        groups = merged
    return groups[0]


def _bisect(s, v):
    c3 = v[7] > s
    c2 = jnp.where(c3, v[11], v[3]) > s
    c1 = jnp.where(c3, jnp.where(c2, v[13], v[9]), jnp.where(c2, v[5], v[1])) > s
    t0 = jnp.where(c3,
                   jnp.where(c2, jnp.where(c1, v[14], v[12]), jnp.where(c1, v[10], v[8])),
                   jnp.where(c2, jnp.where(c1, v[6], v[4]), jnp.where(c1, v[2], v[0])))
    return c3, c2, c1, t0 > s


def _count_greater(s, v):
    c3, c2, c1, c0 = _bisect(s, v)
    count = (jnp.where(c3, 8.0, 0.0) + jnp.where(c2, 4.0, 0.0)
             + jnp.where(c1, 2.0, 0.0) + jnp.where(c0, 1.0, 0.0))
    return jnp.where(v[15] > s, float(PEER_TOPK), count)


def _table_at_count(s, v, table):
    c3, c2, c1, c0 = _bisect(s, v)

    def pick(lo, bits):
        if not bits:
            return table[lo]
        span = 1 << (len(bits) - 1)
        return jnp.where(bits[0], pick(lo + span, bits[1:]), pick(lo, bits[1:]))

    return jnp.where(v[15] > s, 0.0, pick(0, (c3, c2, c1, c0)))


def _swap_sublanes_with_list(tiles):
    rows = lax.broadcasted_iota(jnp.int32, tiles[0].shape, 0)
    t = list(tiles)
    for d in (4, 2, 1):
        keep = (rows & d) == 0
        for i in range(8):
            if i & d == 0:
                x, y = t[i], t[i + d]
                t[i] = jnp.where(keep, x, pltpu.roll(y, d, 0))
                t[i + d] = jnp.where(keep, pltpu.roll(x, 8 - d, 0), y)
    return t


def _route_kernel(hf_ref, wq_ref, sk_ref, rank1_ref, e1_ref, nk_ref, e0_ref,
                  s_scr, rank_scr, val_scr):
    tmr = hf_ref.shape[0]
    q = jnp.dot(hf_ref[...], wq_ref[...], preferred_element_type=F32).astype(BF16)
    scores = [lax.dot_general(sk_ref[hp], q[:, hp * N_KEYS:(hp + 1) * N_KEYS],
                              (((1,), (1,)), ((), ())), preferred_element_type=F32)
              for hp in range(2 * PEER_HEADS)]
    for p in range(2):
        for kb in range(N_KEYS // 8):
            per_key = _swap_sublanes_with_list(
                [scores[2 * h + p][kb * 8:(kb + 1) * 8, :] for h in range(PEER_HEADS)])
            for i in range(8):
                s_scr[p, kb * 8 + i] = per_key[i]

    flawed = jnp.zeros((PEER_HEADS, tmr), F32)
    for p in range(2):
        s = [s_scr[p, k] for k in range(N_KEYS)]
        v = _top16_sorted(s)
        reach = jnp.zeros((PEER_HEADS, tmr), F32)
        for k in range(N_KEYS):
            reach = reach + jnp.where(v[PEER_TOPK - 1] > s[k], 0.0, 1.0)
        flawed = jnp.maximum(flawed, jnp.abs(reach - float(PEER_TOPK)))
        for r in range(PEER_TOPK):
            val_scr[p, r] = v[r]
            if r:
                flawed = jnp.maximum(flawed, jnp.where(v[r - 1] > v[r], 0.0, 1.0))
    tied = jnp.max(flawed) > 0.0

    @pl.when(tied)
    def _():
        for p in range(2):
            for k in range(N_KEYS):
                rank_scr[p, k] = jnp.zeros((PEER_HEADS, tmr), F32)

            def against(kp, carry):
                other = s_scr[p, kp]
                for k in range(N_KEYS):
                    mine = s_scr[p, k]
                    beats = (other > mine) | ((other == mine) & (kp < k))
                    rank_scr[p, k] = rank_scr[p, k] + jnp.where(beats, 1.0, 0.0)
                return carry

            lax.fori_loop(0, N_KEYS, against, 0)
            for r in range(PEER_TOPK):
                v = jnp.zeros((PEER_HEADS, tmr), F32)
                for k in range(N_KEYS):
                    v = jnp.where(rank_scr[p, k] == float(r), s_scr[p, k], v)
                val_scr[p, r] = v

    cells = _young_cells()
    v0 = [val_scr[0, a] for a in range(PEER_TOPK)]
    v1 = [val_scr[1, b] for b in range(PEER_TOPK)]
    csum = {c: v0[c[0]] + v1[c[1]] for c in cells}
    beaten = {c: jnp.full(csum[c].shape, float((c[0] + 1) * (c[1] + 1) - 1), F32) for c in cells}
    for ix, cx in enumerate(cells):
        for cy in cells[ix + 1:]:
            comparable = (cx[0] <= cy[0] and cx[1] <= cy[1]) or (cy[0] <= cx[0] and cy[1] <= cx[1])
            if comparable:
                continue
            y_wins = jnp.where(csum[cy] > csum[cx], 1.0, 0.0)
            beaten[cx] = beaten[cx] + y_wins
            beaten[cy] = beaten[cy] + (1.0 - y_wins)
    top = csum[(0, 0)]
    zsum = jnp.zeros_like(top)
    ncol = [jnp.zeros_like(top) for _ in range(PEER_TOPK)]
    for c in cells:
        chosen = beaten[c] < float(PEER_TOPK)
        zsum = zsum + jnp.where(chosen, jnp.exp(csum[c] - top), 0.0)
        ncol[c[0]] = ncol[c[0]] + jnp.where(chosen, 1.0, 0.0)
    inv_z = 1.0 / zsum

    def emit(partners_of_key, rank_of_key):
        for k in range(N_KEYS):
            nk_ref[k] = partners_of_key(k)
            e0_ref[k] = jnp.exp(s_scr[0, k] - v0[0]) * inv_z
        for kb in range(N_KEYS // PACK):
            rank_rows, e1_rows = [], []
            for half in range(PACK // 8):
                keys = range(kb * PACK + half * 8, kb * PACK + half * 8 + 8)
                rank_rows.append(_swap_sublanes_with_list([rank_of_key(k) for k in keys]))
                e1_rows.append(_swap_sublanes_with_list([jnp.exp(s_scr[1, k] - v1[0]) for k in keys]))
            for h in range(PEER_HEADS):
                rows = slice(kb * PACK, (kb + 1) * PACK)
                rank1_ref[h, rows, :] = jnp.concatenate([part[h] for part in rank_rows], axis=0).astype(BF16)
                e1_ref[h, rows, :] = jnp.concatenate([part[h] for part in e1_rows], axis=0).astype(BF16)

    @pl.when(jnp.logical_not(tied))
    def _():
        emit(lambda k: _table_at_count(s_scr[0, k], v0, ncol),
             lambda k: _count_greater(s_scr[1, k], v1))

    @pl.when(tied)
    def _():
        def partners(k):
            rank0 = rank_scr[0, k]
            nk = jnp.zeros((PEER_HEADS, tmr), F32)
            for a in range(PEER_TOPK):
                nk = jnp.where(rank0 == float(a), ncol[a], nk)
            return nk

        emit(partners, lambda k: rank_scr[1, k])


def _route(hf, wq_bf, sk_bf, tmr):
    t, d = hf.shape
    sk2 = sk_bf.reshape(2 * PEER_HEADS, N_KEYS, sk_bf.shape[-1])
    dense = jax.ShapeDtypeStruct((PEER_HEADS, N_KEYS, t), BF16)
    dense_spec = pl.BlockSpec((PEER_HEADS, N_KEYS, tmr), lambda i: (0, 0, i))
    rows = jax.ShapeDtypeStruct((N_KEYS, PEER_HEADS, t), F32)
    rows_spec = pl.BlockSpec((N_KEYS, PEER_HEADS, tmr), lambda i: (0, 0, i))
    return pl.pallas_call(
        _route_kernel,
        grid=(t // tmr,),
        in_specs=[pl.BlockSpec((tmr, d), lambda i: (i, 0)),
                  pl.BlockSpec(wq_bf.shape, lambda i: (0, 0)),
                  pl.BlockSpec(sk2.shape, lambda i: (0, 0, 0))],
        out_specs=[dense_spec, dense_spec, rows_spec, rows_spec],
        out_shape=[dense, dense, rows, rows],
        scratch_shapes=[pltpu.VMEM((2, N_KEYS, PEER_HEADS, tmr), F32),
                        pltpu.VMEM((2, N_KEYS, PEER_HEADS, tmr), F32),
                        pltpu.VMEM((2, PEER_TOPK, PEER_HEADS, tmr), F32)],
        compiler_params=_cparams(("parallel",)),
        name="route",
    )(hf, wq_bf, sk2)


PACK = 16
GATE_LANES = 256


def _gate_rows(row_ref, h, r, lanes):
    row = row_ref[r, h:h + 1, lanes]
    half = jnp.broadcast_to(row, (PACK // 2, row.shape[1]))
    return jnp.concatenate([half, half], axis=0).astype(BF16)


def _peer_gate_chunk(s_ref, w_ref, rank1_ref, e1_ref, nk_ref, e0_ref):
    ec, tm = s_ref.shape
    subs = N_KEYS // PACK
    for j in range(ec // N_KEYS):
        for g in range(tm // GATE_LANES):
            lanes = slice(g * GATE_LANES, (g + 1) * GATE_LANES)
            gate = [jnp.zeros((PACK, GATE_LANES), BF16) for _ in range(subs)]
            for h in range(PEER_HEADS):
                n_rows = _gate_rows(nk_ref, h, j, lanes)
                e0_rows = _gate_rows(e0_ref, h, j, lanes)
                for k in range(subs):
                    rows = slice(k * PACK, (k + 1) * PACK)
                    chosen = rank1_ref[h, rows, lanes] < n_rows
                    picked = jnp.where(chosen, e1_ref[h, rows, lanes], jnp.zeros((), BF16))
                    gate[k] = gate[k] + picked * e0_rows
            for k in range(subs):
                rows = slice(j * N_KEYS + k * PACK, j * N_KEYS + (k + 1) * PACK)
                s = s_ref[rows, lanes].astype(BF16)
                act = 0.5 * s * (1.0 + lax.erf(s * (0.5 ** 0.5)))
                w_ref[rows, lanes] = act * gate[k]


def _peer_kernel(u_ref, hft_ref, vt_ref, rank1_ref, e1_ref, nk_ref, e0_ref, x1_ref, g_ref, o_ref,
                 s_scr, w_scr, yt_scr, *, final_norm):
    c = pl.program_id(1)

    @pl.when(c == 0)
    def _():
        yt_scr[...] = jnp.zeros_like(yt_scr)

    s_scr[...] = jnp.dot(u_ref[...], hft_ref[...], preferred_element_type=F32)
    _peer_gate_chunk(s_scr, w_scr, rank1_ref, e1_ref, nk_ref, e0_ref)
    yt_scr[...] += jnp.dot(vt_ref[...], w_scr[...], preferred_element_type=F32)

    @pl.when(c == pl.num_programs(1) - 1)
    def _():
        x2 = x1_ref[...] + yt_scr[...].T
        o_ref[...] = _rms(x2, g_ref[...]) if final_norm else x2


def _peer(u_bf, hft, vt_bf, rank1, e1, nk, e0, x1, g, final_norm, tm, ec):
    n_exp, d = u_bf.shape
    t = hft.shape[1]
    rspec = pl.BlockSpec((PEER_HEADS, N_KEYS, tm), lambda i, c: (0, 0, i))
    rows = pl.BlockSpec((ec // N_KEYS, PEER_HEADS, tm), lambda i, c: (c, 0, i))
    return pl.pallas_call(
        functools.partial(_peer_kernel, final_norm=final_norm),
        grid=(t // tm, n_exp // ec),
        in_specs=[pl.BlockSpec((ec, d), lambda i, c: (c, 0)),
                  pl.BlockSpec((d, tm), lambda i, c: (0, i)),
                  pl.BlockSpec((d, ec), lambda i, c: (0, c)),
                  rspec, rspec, rows, rows,
                  pl.BlockSpec((tm, d), lambda i, c: (i, 0)),
                  pl.BlockSpec((1, d), lambda i, c: (0, 0))],
        out_specs=pl.BlockSpec((tm, d), lambda i, c: (i, 0)),
        out_shape=jax.ShapeDtypeStruct((t, d), F32),
        scratch_shapes=[pltpu.VMEM((ec, tm), F32), pltpu.VMEM((ec, tm), BF16),
                        pltpu.VMEM((d, tm), F32)],
        compiler_params=_cparams(("parallel", "arbitrary")),
        name="peer",
    )(u_bf, hft, vt_bf, rank1, e1, nk, e0, x1, g)


def _pick(total, want):
    tile = min(total, want)
    assert total % tile == 0, (total, tile)
    return tile


def kernel(x, norm_mix, w_in, w_pool, pool_scale, w_out, norm_ffn, w_query, sub_keys,
           expert_u, expert_v, norm_final):
    b, s, d = x.shape
    t = b * s
    depth = norm_mix.shape[0]
    d_pool = w_pool.shape[1] * w_pool.shape[2]
    d_qkv = w_in.shape[2] - d_pool
    d_attn = d_qkv // 3
    assert d_attn == w_out.shape[1] - d_pool and d_attn % LANES == 0
    assert sub_keys.shape[1:] == (PEER_HEADS, 2, N_KEYS, N_KEYS)
    assert expert_u.shape[1] == N_KEYS * N_KEYS
    assert s % (DILATED_PATTERNS[-1][1] * WBLK) == 0

    tm = _pick(s, 512)
    tm_route = _pick(s, 128)
    tm_peer = _pick(s, 1024)
    ec = 1024

    x2 = x.reshape(t, d)
    for layer in range(depth):
        qkv, zp = _in_proj(x2, norm_mix[layer][None], w_in[layer].astype(BF16), d_qkv, tm)
        att = _attention(qkv, b, s)
        x1, hf, hft = _mix_out(att, zp, x2, w_pool[layer].astype(BF16),
                               pool_scale[layer][None], w_out[layer].astype(BF16),
                               norm_ffn[layer][None], s, tm)
        rank1, e1, nk, e0 = _route(hf, w_query[layer].astype(BF16), sub_keys[layer].astype(BF16),
                                   tm_route)
        last = layer + 1 == depth
        x2 = _peer(expert_u[layer].astype(BF16), hft, expert_v[layer].astype(BF16).T,
                   rank1, e1, nk, e0, x1, norm_final[None], last, tm_peer, ec)
    return x2.reshape(b, s, d)
```

```python
import functools

import jax
import jax.numpy as jnp
from jax import lax
from jax.experimental import pallas as pl
from jax.experimental.pallas import tpu as pltpu

F32 = jnp.float32
BF16 = jnp.bfloat16

EPS = 1e-6
NEG = -1e30
HEAD_DIM = 64
LANES = 128
DILATED_PATTERNS = ((128, 1), (512, 4), (2048, 16))
WBLK = 128
POOL_WINDOWS = (2, 4, 8, 16)
POOL_HALO = 16
PEER_HEADS = 8
N_KEYS = 128
PEER_TOPK = 16
VMEM_LIMIT = 56 * 1024 * 1024


def _cparams(sem):
    return pltpu.CompilerParams(dimension_semantics=sem, vmem_limit_bytes=VMEM_LIMIT)


def _rms(x, g):
    ms = jnp.mean(x * x, axis=-1, keepdims=True)
    return x * lax.rsqrt(ms + EPS) * g


def _inproj_kernel(x_ref, g_ref, w_ref, qkv_ref, zp_ref):
    h = _rms(x_ref[...], g_ref[...]).astype(BF16)
    z = jnp.dot(h, w_ref[...], preferred_element_type=F32)
    n_blocks = qkv_ref.shape[0]
    for j in range(n_blocks):
        qkv_ref[j] = z[:, j * LANES:(j + 1) * LANES]
    zp_ref[...] = z[:, n_blocks * LANES:]


def _in_proj(x2, g, w_bf, d_qkv, tm):
    t, d = x2.shape
    e = w_bf.shape[1]
    n_blocks = d_qkv // LANES
    return pl.pallas_call(
        _inproj_kernel,
        grid=(t // tm,),
        in_specs=[pl.BlockSpec((tm, d), lambda i: (i, 0)),
                  pl.BlockSpec((1, d), lambda i: (0, 0)),
                  pl.BlockSpec((d, e), lambda i: (0, 0))],
        out_specs=[pl.BlockSpec((n_blocks, tm, LANES), lambda i: (0, i, 0)),
                   pl.BlockSpec((tm, e - d_qkv), lambda i: (i, 0))],
        out_shape=[jax.ShapeDtypeStruct((n_blocks, t, LANES), F32),
                   jax.ShapeDtypeStruct((t, e - d_qkv), F32)],
        compiler_params=_cparams(("parallel",)),
        name="in_proj",
    )(x2, g, w_bf)


ATTN_TILE = DILATED_PATTERNS[-1][1] * WBLK


def _rows(ref, start, size, stride):
    if stride == 1:
        return ref[start:start + size, :]
    return ref[pl.ds(start, size, stride=stride), :]


def _attn_block(q, kk, vv1, mask2):
    lane_a = lax.broadcasted_iota(jnp.int32, q.shape, 1) < HEAD_DIM
    zero = jnp.zeros((), q.dtype)
    q2 = jnp.concatenate([jnp.where(lane_a, q, zero), jnp.where(lane_a, zero, q)], axis=0)
    s = lax.dot_general(q2, kk, (((1,), (1,)), ((), ())), preferred_element_type=F32)
    s = jnp.where(mask2, s * (HEAD_DIM ** -0.5), NEG)
    m = jnp.max(s, axis=-1, keepdims=True)
    p = jnp.exp(s - m).astype(BF16)
    pv = jnp.dot(p, vv1, preferred_element_type=F32)
    w = q.shape[0]
    num = jnp.where(lane_a, pv[:w, :LANES], pv[w:, :LANES])
    den = jnp.where(lane_a, pv[:w, LANES:], pv[w:, LANES:])
    mm = jnp.where(lane_a, m[:w], m[w:])
    return num / den, mm + jnp.log(den)


def _attn_kernel(q_ref, kp_ref, kc_ref, vp_ref, vc_ref, att_ref, o_scr, l_scr):
    n = pl.program_id(2)
    tile = q_ref.shape[0]
    qi = lax.broadcasted_iota(jnp.int32, (2 * WBLK, 2 * WBLK), 0) % WBLK
    kj = lax.broadcasted_iota(jnp.int32, (2 * WBLK, 2 * WBLK), 1)
    dist = qi + WBLK - kj
    for bi, (window, dil) in enumerate(DILATED_PATTERNS):
        band = (dist >= 0) & (dist <= window // dil)
        band_first = band & ((kj >= WBLK) | (n > 0))
        sub = tile // dil
        ones = jnp.ones((WBLK + sub, LANES), BF16)
        for c in range(dil):
            q = _rows(q_ref, c, sub, dil).astype(BF16)
            halo = tile - WBLK * dil + c
            kcat = jnp.concatenate([_rows(kp_ref, halo, WBLK, dil), _rows(kc_ref, c, sub, dil)],
                                   axis=0).astype(BF16)
            vcat = jnp.concatenate([_rows(vp_ref, halo, WBLK, dil), _rows(vc_ref, c, sub, dil)],
                                   axis=0).astype(BF16)
            vcat = jnp.concatenate([vcat, ones], axis=1)
            for j in range(sub // WBLK):
                o, lse = _attn_block(q[j * WBLK:(j + 1) * WBLK], kcat[j * WBLK:(j + 2) * WBLK],
                                     vcat[j * WBLK:(j + 2) * WBLK], band_first if j == 0 else band)
                first_row = c + dil * j * WBLK
                if dil == 1:
                    o_scr[bi, first_row:first_row + WBLK, :] = o
                    l_scr[bi, first_row:first_row + WBLK, :] = lse
                else:
                    o_scr[bi, pl.ds(first_row, WBLK, stride=dil), :] = o
                    l_scr[bi, pl.ds(first_row, WBLK, stride=dil), :] = lse

    l1, l2, l3 = l_scr[0], l_scr[1], l_scr[2]
    lm = jnp.maximum(jnp.maximum(l1, l2), l3)
    e1, e2, e3 = jnp.exp(l1 - lm), jnp.exp(l2 - lm), jnp.exp(l3 - lm)
    att_ref[...] = (e1 * o_scr[0] + e2 * o_scr[1] + e3 * o_scr[2]) / (e1 + e2 + e3)


def _attention(qkv, b, s):
    hp = qkv.shape[0] // 3
    tile = ATTN_TILE
    qkv5 = qkv.reshape(3 * hp, b, s, LANES)
    blk = (None, None, tile, LANES)

    def spec(which, prev):
        if prev:
            return pl.BlockSpec(blk, lambda bi, hi, ni: (which * hp + hi, bi, jnp.maximum(ni - 1, 0), 0))
        return pl.BlockSpec(blk, lambda bi, hi, ni: (which * hp + hi, bi, ni, 0))

    att = pl.pallas_call(
        _attn_kernel,
        grid=(b, hp, s // tile),
        in_specs=[spec(0, False), spec(1, True), spec(1, False), spec(2, True), spec(2, False)],
        out_specs=pl.BlockSpec(blk, lambda bi, hi, ni: (hi, bi, ni, 0)),
        out_shape=jax.ShapeDtypeStruct((hp, b, s, LANES), F32),
        scratch_shapes=[pltpu.VMEM((len(DILATED_PATTERNS), tile, LANES), F32),
                        pltpu.VMEM((len(DILATED_PATTERNS), tile, LANES), F32)],
        compiler_params=_cparams(("parallel", "parallel", "arbitrary")),
        name="attention",
    )(qkv5, qkv5, qkv5, qkv5, qkv5)
    return att.reshape(hp, b * s, LANES)


def _mixout_kernel(att_ref, zp_ref, zh_ref, x_ref, wp_ref, ps_ref, wo_ref, g_ref, wq_ref,
                   x1_ref, q_ref, hft_ref, *, tiles_per_seq):
    i = pl.program_id(0)
    tm = x_ref.shape[0]
    first = (i % tiles_per_seq) == 0
    att = jnp.concatenate([att_ref[j] for j in range(att_ref.shape[0])], axis=1)

    zc = zp_ref[...]
    halo = jnp.where(first, 0.0, zh_ref[...])
    buf = jnp.concatenate([zc, halo], axis=0)
    sums = {1: buf}
    w = 1
    while w < POOL_WINDOWS[-1]:
        sums[2 * w] = sums[w] + pltpu.roll(sums[w], w, 0)
        w *= 2
    pos = (i % tiles_per_seq) * tm + lax.broadcasted_iota(jnp.int32, (tm, 1), 0) + 1
    cg = zc.shape[1] // len(POOL_WINDOWS)
    mixed = []
    for g, win in enumerate(POOL_WINDOWS):
        cols = slice(g * cg, (g + 1) * cg)
        cnt = jnp.minimum(pos, win).astype(F32)
        pooled = sums[win][:tm, cols] / cnt - zc[:, cols]
        mixed.append(jnp.dot(pooled.astype(BF16), wp_ref[g], preferred_element_type=F32))
    mixed = jnp.concatenate(mixed, axis=1) * ps_ref[...]

    mix = jnp.concatenate([att, mixed], axis=1).astype(BF16)
    x1 = x_ref[...] + jnp.dot(mix, wo_ref[...], preferred_element_type=F32)
    x1_ref[...] = x1
    hf = _rms(x1, g_ref[...])
    q_ref[...] = jnp.dot(hf.astype(BF16), wq_ref[...], preferred_element_type=F32).astype(BF16)
    hft_ref[...] = hf.T.astype(BF16)


def _mix_out(att, zp, x2, wp_bf, ps, wo_bf, g, wq_bf, seq, tm):
    t, d = x2.shape
    dp = zp.shape[1]
    dq = wq_bf.shape[1]
    row = lambda i: (i, 0)
    const2 = lambda i: (0, 0)
    halo_rows = tm // POOL_HALO
    return pl.pallas_call(
        functools.partial(_mixout_kernel, tiles_per_seq=seq // tm),
        grid=(t // tm,),
        in_specs=[
            pl.BlockSpec((att.shape[0], tm, LANES), lambda i: (0, i, 0)),
            pl.BlockSpec((tm, dp), row),
            pl.BlockSpec((POOL_HALO, dp), lambda i: (jnp.maximum(i * halo_rows - 1, 0), 0)),
            pl.BlockSpec((tm, d), row),
            pl.BlockSpec(wp_bf.shape, lambda i: (0, 0, 0)),
            pl.BlockSpec((1, dp), const2),
            pl.BlockSpec(wo_bf.shape, const2),
            pl.BlockSpec((1, d), const2),
            pl.BlockSpec(wq_bf.shape, const2)],
        out_specs=[pl.BlockSpec((tm, d), row), pl.BlockSpec((tm, dq), row),
                   pl.BlockSpec((d, tm), lambda i: (0, i))],
        out_shape=[jax.ShapeDtypeStruct((t, d), F32), jax.ShapeDtypeStruct((t, dq), BF16),
                   jax.ShapeDtypeStruct((d, t), BF16)],
        compiler_params=_cparams(("parallel",)),
        name="mix_out",
    )(att, zp, zp, x2, wp_bf, ps, wo_bf, g, wq_bf)


def _young_cells():
    return [(a, b) for a in range(PEER_TOPK) for b in range(PEER_TOPK)
            if (a + 1) * (b + 1) <= PEER_TOPK]


def _sort16_pairs():
    n, pairs, p = PEER_TOPK, [], 1
    while p < n:
        k = p
        while k >= 1:
            for j in range(k % p, n - k, 2 * k):
                for i in range(min(k, n - j - k)):
                    if (i + j) // (2 * p) == (i + j + k) // (2 * p):
                        pairs.append((i + j, i + j + k))
            k //= 2
        p *= 2
    return pairs


def _top16_sorted(vals):
    groups = []
    for g in range(0, len(vals), PEER_TOPK):
        x = list(vals[g:g + PEER_TOPK])
        for a, b in _sort16_pairs():
            x[a], x[b] = jnp.maximum(x[a], x[b]), jnp.minimum(x[a], x[b])
        groups.append(x)
    while len(groups) > 1:
        merged = []
        for x, y in zip(groups[0::2], groups[1::2]):
            z = [jnp.maximum(x[i], y[PEER_TOPK - 1 - i]) for i in range(PEER_TOPK)]
            d = PEER_TOPK // 2
            while d >= 1:
                for i in range(PEER_TOPK):
                    if i & d == 0:
                        z[i], z[i + d] = jnp.maximum(z[i], z[i + d]), jnp.minimum(z[i], z[i + d])
                d //= 2
            merged.append(z)
        groups = merged
    return groups[0]


def _bisect(s, v):
    c3 = v[7] > s
    c2 = jnp.where(c3, v[11], v[3]) > s
    c1 = jnp.where(c3, jnp.where(c2, v[13], v[9]), jnp.where(c2, v[5], v[1])) > s
    t0 = jnp.where(c3,
                   jnp.where(c2, jnp.where(c1, v[14], v[12]), jnp.where(c1, v[10], v[8])),
                   jnp.where(c2, jnp.where(c1, v[6], v[4]), jnp.where(c1, v[2], v[0])))
    return c3, c2, c1, t0 > s


def _count_greater(s, v):
    c3, c2, c1, c0 = _bisect(s, v)
    count = (jnp.where(c3, 8.0, 0.0) + jnp.where(c2, 4.0, 0.0)
             + jnp.where(c1, 2.0, 0.0) + jnp.where(c0, 1.0, 0.0))
    return jnp.where(v[15] > s, float(PEER_TOPK), count)


def _table_at_count(s, v, table):
    c3, c2, c1, c0 = _bisect(s, v)

    def pick(lo, bits):
        if not bits:
            return table[lo]
        span = 1 << (len(bits) - 1)
        return jnp.where(bits[0], pick(lo + span, bits[1:]), pick(lo, bits[1:]))

    return jnp.where(v[15] > s, 0.0, pick(0, (c3, c2, c1, c0)))


def _swap_sublanes_with_list(tiles):
    rows = lax.broadcasted_iota(jnp.int32, tiles[0].shape, 0)
    t = list(tiles)
    for d in (4, 2, 1):
        keep = (rows & d) == 0
        for i in range(8):
            if i & d == 0:
                x, y = t[i], t[i + d]
                t[i] = jnp.where(keep, x, pltpu.roll(y, d, 0))
                t[i + d] = jnp.where(keep, pltpu.roll(x, 8 - d, 0), y)
    return t


def _route_kernel(q_ref, sk_ref, rank1_ref, e1_ref, nk_ref, e0_ref,
                  s_scr, rank_scr, val_scr):
    tmr = q_ref.shape[0]
    q = q_ref[...]
    scores = [lax.dot_general(sk_ref[hp], q[:, hp * N_KEYS:(hp + 1) * N_KEYS],
                              (((1,), (1,)), ((), ())), preferred_element_type=F32)
              for hp in range(2 * PEER_HEADS)]
    for p in range(2):
        for kb in range(N_KEYS // 8):
            per_key = _swap_sublanes_with_list(
                [scores[2 * h + p][kb * 8:(kb + 1) * 8, :] for h in range(PEER_HEADS)])
            for i in range(8):
                s_scr[p, kb * 8 + i] = per_key[i]

    flawed = jnp.zeros((PEER_HEADS, tmr), F32)
    for p in range(2):
        s = [s_scr[p, k] for k in range(N_KEYS)]
        v = _top16_sorted(s)
        reach = jnp.zeros((PEER_HEADS, tmr), F32)
        for k in range(N_KEYS):
            reach = reach + jnp.where(v[PEER_TOPK - 1] > s[k], 0.0, 1.0)
        flawed = jnp.maximum(flawed, jnp.abs(reach - float(PEER_TOPK)))
        for r in range(PEER_TOPK):
            val_scr[p, r] = v[r]
            if r:
                flawed = jnp.maximum(flawed, jnp.where(v[r - 1] > v[r], 0.0, 1.0))
    tied = jnp.max(flawed) > 0.0

    @pl.when(tied)
    def _():
        for p in range(2):
            for k in range(N_KEYS):
                rank_scr[p, k] = jnp.zeros((PEER_HEADS, tmr), F32)

            def against(kp, carry):
                other = s_scr[p, kp]
                for k in range(N_KEYS):
                    mine = s_scr[p, k]
                    beats = (other > mine) | ((other == mine) & (kp < k))
                    rank_scr[p, k] = rank_scr[p, k] + jnp.where(beats, 1.0, 0.0)
                return carry

            lax.fori_loop(0, N_KEYS, against, 0)
            for r in range(PEER_TOPK):
                v = jnp.zeros((PEER_HEADS, tmr), F32)
                for k in range(N_KEYS):
                    v = jnp.where(rank_scr[p, k] == float(r), s_scr[p, k], v)
                val_scr[p, r] = v

    cells = _young_cells()
    v0 = [val_scr[0, a] for a in range(PEER_TOPK)]
    v1 = [val_scr[1, b] for b in range(PEER_TOPK)]
    csum = {c: v0[c[0]] + v1[c[1]] for c in cells}
    beaten = {c: jnp.full(csum[c].shape, float((c[0] + 1) * (c[1] + 1) - 1), F32) for c in cells}
    for ix, cx in enumerate(cells):
        for cy in cells[ix + 1:]:
            comparable = (cx[0] <= cy[0] and cx[1] <= cy[1]) or (cy[0] <= cx[0] and cy[1] <= cx[1])
            if comparable:
                continue
            y_wins = jnp.where(csum[cy] > csum[cx], 1.0, 0.0)
            beaten[cx] = beaten[cx] + y_wins
            beaten[cy] = beaten[cy] + (1.0 - y_wins)
    top = csum[(0, 0)]
    zsum = jnp.zeros_like(top)
    ncol = [jnp.zeros_like(top) for _ in range(PEER_TOPK)]
    for c in cells:
        chosen = beaten[c] < float(PEER_TOPK)
        zsum = zsum + jnp.where(chosen, jnp.exp(csum[c] - top), 0.0)
        ncol[c[0]] = ncol[c[0]] + jnp.where(chosen, 1.0, 0.0)
    inv_z = 1.0 / zsum

    def emit(partners_of_key, rank_of_key):
        for k in range(N_KEYS):
            nk_ref[k] = partners_of_key(k)
            e0_ref[k] = jnp.exp(s_scr[0, k] - v0[0]) * inv_z
        for kb in range(N_KEYS // PACK):
            rank_rows, e1_rows = [], []
            for half in range(PACK // 8):
                keys = range(kb * PACK + half * 8, kb * PACK + half * 8 + 8)
                rank_rows.append(_swap_sublanes_with_list([rank_of_key(k) for k in keys]))
                e1_rows.append(_swap_sublanes_with_list([jnp.exp(s_scr[1, k] - v1[0]) for k in keys]))
            for h in range(PEER_HEADS):
                rows = slice(kb * PACK, (kb + 1) * PACK)
                rank1_ref[h, rows, :] = jnp.concatenate([part[h] for part in rank_rows], axis=0).astype(BF16)
                e1_ref[h, rows, :] = jnp.concatenate([part[h] for part in e1_rows], axis=0).astype(BF16)

    @pl.when(jnp.logical_not(tied))
    def _():
        emit(lambda k: _table_at_count(s_scr[0, k], v0, ncol),
             lambda k: _count_greater(s_scr[1, k], v1))

    @pl.when(tied)
    def _():
        def partners(k):
            rank0 = rank_scr[0, k]
            nk = jnp.zeros((PEER_HEADS, tmr), F32)
            for a in range(PEER_TOPK):
                nk = jnp.where(rank0 == float(a), ncol[a], nk)
            return nk

        emit(partners, lambda k: rank_scr[1, k])


def _route(q, sk_bf, tmr):
    t, d = q.shape
    sk2 = sk_bf.reshape(2 * PEER_HEADS, N_KEYS, sk_bf.shape[-1])
    dense = jax.ShapeDtypeStruct((PEER_HEADS, N_KEYS, t), BF16)
    dense_spec = pl.BlockSpec((PEER_HEADS, N_KEYS, tmr), lambda i: (0, 0, i))
    rows = jax.ShapeDtypeStruct((N_KEYS, PEER_HEADS, t), F32)
    rows_spec = pl.BlockSpec((N_KEYS, PEER_HEADS, tmr), lambda i: (0, 0, i))
    return pl.pallas_call(
        _route_kernel,
        grid=(t // tmr,),
        in_specs=[pl.BlockSpec((tmr, d), lambda i: (i, 0)),
                  pl.BlockSpec(sk2.shape, lambda i: (0, 0, 0))],
        out_specs=[dense_spec, dense_spec, rows_spec, rows_spec],
        out_shape=[dense, dense, rows, rows],
        scratch_shapes=[pltpu.VMEM((2, N_KEYS, PEER_HEADS, tmr), F32),
                        pltpu.VMEM((2, N_KEYS, PEER_HEADS, tmr), F32),
                        pltpu.VMEM((2, PEER_TOPK, PEER_HEADS, tmr), F32)],
        compiler_params=_cparams(("parallel",)),
        name="route",
    )(q, sk2)


PACK = 16
GATE_LANES = 256


def _gate_rows(row_ref, h, r, lanes):
    row = row_ref[r, h:h + 1, lanes]
    half = jnp.broadcast_to(row, (PACK // 2, row.shape[1]))
    return jnp.concatenate([half, half], axis=0).astype(BF16)


def _peer_gate_chunk(s_ref, w_ref, rank1_ref, e1_ref, nk_ref, e0_ref):
    ec, tm = s_ref.shape
    subs = N_KEYS // PACK
    for j in range(ec // N_KEYS):
        for g in range(tm // GATE_LANES):
            lanes = slice(g * GATE_LANES, (g + 1) * GATE_LANES)
            gate = [jnp.zeros((PACK, GATE_LANES), BF16) for _ in range(subs)]
            for h in range(PEER_HEADS):
                n_rows = _gate_rows(nk_ref, h, j, lanes)
                e0_rows = _gate_rows(e0_ref, h, j, lanes)
                for k in range(subs):
                    rows = slice(k * PACK, (k + 1) * PACK)
                    chosen = rank1_ref[h, rows, lanes] < n_rows
                    picked = jnp.where(chosen, e1_ref[h, rows, lanes], jnp.zeros((), BF16))
                    gate[k] = gate[k] + picked * e0_rows
            for k in range(subs):
                rows = slice(j * N_KEYS + k * PACK, j * N_KEYS + (k + 1) * PACK)
                s = s_ref[rows, lanes].astype(BF16)
                act = 0.5 * s * (1.0 + lax.erf(s * (0.5 ** 0.5)))
                w_ref[rows, lanes] = act * gate[k]


def _peer_kernel(u_ref, hft_ref, vt_ref, rank1_ref, e1_ref, nk_ref, e0_ref, x1_ref, g_ref, o_ref,
                 s_scr, w_scr, yt_scr, *, final_norm):
    c = pl.program_id(1)

    @pl.when(c == 0)
    def _():
        yt_scr[...] = jnp.zeros_like(yt_scr)

    s_scr[...] = jnp.dot(u_ref[...], hft_ref[...], preferred_element_type=F32)
    _peer_gate_chunk(s_scr, w_scr, rank1_ref, e1_ref, nk_ref, e0_ref)
    yt_scr[...] += jnp.dot(vt_ref[...], w_scr[...], preferred_element_type=F32)

    @pl.when(c == pl.num_programs(1) - 1)
    def _():
        x2 = x1_ref[...] + yt_scr[...].T
        o_ref[...] = _rms(x2, g_ref[...]) if final_norm else x2


def _peer(u_bf, hft, vt_bf, rank1, e1, nk, e0, x1, g, final_norm, tm, ec):
    n_exp, d = u_bf.shape
    t = hft.shape[1]
    rspec = pl.BlockSpec((PEER_HEADS, N_KEYS, tm), lambda i, c: (0, 0, i))
    rows = pl.BlockSpec((ec // N_KEYS, PEER_HEADS, tm), lambda i, c: (c, 0, i))
    return pl.pallas_call(
        functools.partial(_peer_kernel, final_norm=final_norm),
        grid=(t // tm, n_exp // ec),
        in_specs=[pl.BlockSpec((ec, d), lambda i, c: (c, 0)),
                  pl.BlockSpec((d, tm), lambda i, c: (0, i)),
                  pl.BlockSpec((d, ec), lambda i, c: (0, c)),
                  rspec, rspec, rows, rows,
                  pl.BlockSpec((tm, d), lambda i, c: (i, 0)),
                  pl.BlockSpec((1, d), lambda i, c: (0, 0))],
        out_specs=pl.BlockSpec((tm, d), lambda i, c: (i, 0)),
        out_shape=jax.ShapeDtypeStruct((t, d), F32),
        scratch_shapes=[pltpu.VMEM((ec, tm), F32), pltpu.VMEM((ec, tm), BF16),
                        pltpu.VMEM((d, tm), F32)],
        compiler_params=_cparams(("parallel", "arbitrary")),
        name="peer",
    )(u_bf, hft, vt_bf, rank1, e1, nk, e0, x1, g)


def _pick(total, want):
    tile = min(total, want)
    assert total % tile == 0, (total, tile)
    return tile


def kernel(x, norm_mix, w_in, w_pool, pool_scale, w_out, norm_ffn, w_query, sub_keys,
           expert_u, expert_v, norm_final):
    b, s, d = x.shape
    t = b * s
    depth = norm_mix.shape[0]
    d_pool = w_pool.shape[1] * w_pool.shape[2]
    d_qkv = w_in.shape[2] - d_pool
    d_attn = d_qkv // 3
    assert d_attn == w_out.shape[1] - d_pool and d_attn % LANES == 0
    assert sub_keys.shape[1:] == (PEER_HEADS, 2, N_KEYS, N_KEYS)
    assert expert_u.shape[1] == N_KEYS * N_KEYS
    assert s % (DILATED_PATTERNS[-1][1] * WBLK) == 0

    tm = _pick(s, 512)
    tm_route = _pick(s, 128)
    tm_peer = _pick(s, 1024)
    ec = 1024

    x2 = x.reshape(t, d)
    for layer in range(depth):
        qkv, zp = _in_proj(x2, norm_mix[layer][None], w_in[layer].astype(BF16), d_qkv, tm)
        att = _attention(qkv, b, s)
        x1, q, hft = _mix_out(att, zp, x2, w_pool[layer].astype(BF16),
                              pool_scale[layer][None], w_out[layer].astype(BF16),
                              norm_ffn[layer][None], w_query[layer].astype(BF16), s, tm)
        rank1, e1, nk, e0 = _route(q, sub_keys[layer].astype(BF16), tm_route)
        last = layer + 1 == depth
        x2 = _peer(expert_u[layer].astype(BF16), hft, expert_v[layer].astype(BF16).T,
                   rank1, e1, nk, e0, x1, norm_final[None], last, tm_peer, ec)
    return x2.reshape(b, s, d)
```

```python
import functools

import jax
import jax.numpy as jnp
from jax import lax
from jax.experimental import pallas as pl
from jax.experimental.pallas import tpu as pltpu

F32 = jnp.float32
BF16 = jnp.bfloat16

EPS = 1e-6
NEG = -1e30
HEAD_DIM = 64
LANES = 128
DILATED_PATTERNS = ((128, 1), (512, 4), (2048, 16))
WBLK = 128
POOL_WINDOWS = (2, 4, 8, 16)
POOL_HALO = 16
PEER_HEADS = 8
N_KEYS = 128
PEER_TOPK = 16
VMEM_LIMIT = 56 * 1024 * 1024


def _cparams(sem):
    return pltpu.CompilerParams(dimension_semantics=sem, vmem_limit_bytes=VMEM_LIMIT)


def _rms(x, g):
    ms = jnp.mean(x * x, axis=-1, keepdims=True)
    return x * lax.rsqrt(ms + EPS) * g


def _inproj_kernel(x_ref, g_ref, w_ref, qkv_ref, zp_ref):
    h = _rms(x_ref[...], g_ref[...]).astype(BF16)
    z = jnp.dot(h, w_ref[...], preferred_element_type=F32)
    n_blocks = qkv_ref.shape[0]
    for j in range(n_blocks):
        qkv_ref[j] = z[:, j * LANES:(j + 1) * LANES]
    zp_ref[...] = z[:, n_blocks * LANES:]


def _in_proj(x2, g, w_bf, d_qkv, tm):
    t, d = x2.shape
    e = w_bf.shape[1]
    n_blocks = d_qkv // LANES
    return pl.pallas_call(
        _inproj_kernel,
        grid=(t // tm,),
        in_specs=[pl.BlockSpec((tm, d), lambda i: (i, 0)),
                  pl.BlockSpec((1, d), lambda i: (0, 0)),
                  pl.BlockSpec((d, e), lambda i: (0, 0))],
        out_specs=[pl.BlockSpec((n_blocks, tm, LANES), lambda i: (0, i, 0)),
                   pl.BlockSpec((tm, e - d_qkv), lambda i: (i, 0))],
        out_shape=[jax.ShapeDtypeStruct((n_blocks, t, LANES), F32),
                   jax.ShapeDtypeStruct((t, e - d_qkv), F32)],
        compiler_params=_cparams(("parallel",)),
        name="in_proj",
    )(x2, g, w_bf)


ATTN_TILE = DILATED_PATTERNS[-1][1] * WBLK


def _rows(ref, start, size, stride):
    if stride == 1:
        return ref[start:start + size, :]
    return ref[pl.ds(start, size, stride=stride), :]


def _attn_block(q, kk, vv1, mask2):
    lane_a = lax.broadcasted_iota(jnp.int32, q.shape, 1) < HEAD_DIM
    zero = jnp.zeros((), q.dtype)
    q2 = jnp.concatenate([jnp.where(lane_a, q, zero), jnp.where(lane_a, zero, q)], axis=0)
    s = lax.dot_general(q2, kk, (((1,), (1,)), ((), ())), preferred_element_type=F32)
    s = jnp.where(mask2, s * (HEAD_DIM ** -0.5), NEG)
    m = jnp.max(s, axis=-1, keepdims=True)
    p = jnp.exp(s - m).astype(BF16)
    pv = jnp.dot(p, vv1, preferred_element_type=F32)
    w = q.shape[0]
    num = jnp.where(lane_a, pv[:w, :LANES], pv[w:, :LANES])
    den = jnp.where(lane_a, pv[:w, LANES:], pv[w:, LANES:])
    mm = jnp.where(lane_a, m[:w], m[w:])
    return num / den, mm + jnp.log(den)


def _attn_kernel(q_ref, kp_ref, kc_ref, vp_ref, vc_ref, att_ref, o_scr, l_scr):
    n = pl.program_id(2)
    tile = q_ref.shape[0]
    qi = lax.broadcasted_iota(jnp.int32, (2 * WBLK, 2 * WBLK), 0) % WBLK
    kj = lax.broadcasted_iota(jnp.int32, (2 * WBLK, 2 * WBLK), 1)
    dist = qi + WBLK - kj
    for bi, (window, dil) in enumerate(DILATED_PATTERNS):
        band = (dist >= 0) & (dist <= window // dil)
        band_first = band & ((kj >= WBLK) | (n > 0))
        sub = tile // dil
        ones = jnp.ones((WBLK + sub, LANES), BF16)
        for c in range(dil):
            q = _rows(q_ref, c, sub, dil).astype(BF16)
            halo = tile - WBLK * dil + c
            kcat = jnp.concatenate([_rows(kp_ref, halo, WBLK, dil), _rows(kc_ref, c, sub, dil)],
                                   axis=0).astype(BF16)
            vcat = jnp.concatenate([_rows(vp_ref, halo, WBLK, dil), _rows(vc_ref, c, sub, dil)],
                                   axis=0).astype(BF16)
            vcat = jnp.concatenate([vcat, ones], axis=1)
            for j in range(sub // WBLK):
                o, lse = _attn_block(q[j * WBLK:(j + 1) * WBLK], kcat[j * WBLK:(j + 2) * WBLK],
                                     vcat[j * WBLK:(j + 2) * WBLK], band_first if j == 0 else band)
                first_row = c + dil * j * WBLK
                if dil == 1:
                    o_scr[bi, first_row:first_row + WBLK, :] = o
                    l_scr[bi, first_row:first_row + WBLK, :] = lse
                else:
                    o_scr[bi, pl.ds(first_row, WBLK, stride=dil), :] = o
                    l_scr[bi, pl.ds(first_row, WBLK, stride=dil), :] = lse

    l1, l2, l3 = l_scr[0], l_scr[1], l_scr[2]
    lm = jnp.maximum(jnp.maximum(l1, l2), l3)
    e1, e2, e3 = jnp.exp(l1 - lm), jnp.exp(l2 - lm), jnp.exp(l3 - lm)
    att_ref[...] = (e1 * o_scr[0] + e2 * o_scr[1] + e3 * o_scr[2]) / (e1 + e2 + e3)


def _attention(qkv, b, s):
    hp = qkv.shape[0] // 3
    tile = ATTN_TILE
    qkv5 = qkv.reshape(3 * hp, b, s, LANES)
    blk = (None, None, tile, LANES)

    def spec(which, prev):
        if prev:
            return pl.BlockSpec(blk, lambda bi, hi, ni: (which * hp + hi, bi, jnp.maximum(ni - 1, 0), 0))
        return pl.BlockSpec(blk, lambda bi, hi, ni: (which * hp + hi, bi, ni, 0))

    att = pl.pallas_call(
        _attn_kernel,
        grid=(b, hp, s // tile),
        in_specs=[spec(0, False), spec(1, True), spec(1, False), spec(2, True), spec(2, False)],
        out_specs=pl.BlockSpec(blk, lambda bi, hi, ni: (hi, bi, ni, 0)),
        out_shape=jax.ShapeDtypeStruct((hp, b, s, LANES), F32),
        scratch_shapes=[pltpu.VMEM((len(DILATED_PATTERNS), tile, LANES), F32),
                        pltpu.VMEM((len(DILATED_PATTERNS), tile, LANES), F32)],
        compiler_params=_cparams(("parallel", "parallel", "arbitrary")),
        name="attention",
    )(qkv5, qkv5, qkv5, qkv5, qkv5)
    return att.reshape(hp, b * s, LANES)


def _mixout_kernel(att_ref, zp_ref, zh_ref, x_ref, wp_ref, ps_ref, wo_ref, g_ref, wq_ref,
                   x1_ref, q_ref, hft_ref, *, tiles_per_seq):
    i = pl.program_id(0)
    tm = x_ref.shape[0]
    first = (i % tiles_per_seq) == 0
    att = jnp.concatenate([att_ref[j] for j in range(att_ref.shape[0])], axis=1)

    zc = zp_ref[...]
    halo = jnp.where(first, 0.0, zh_ref[...])
    buf = jnp.concatenate([zc, halo], axis=0)
    sums = {1: buf}
    w = 1
    while w < POOL_WINDOWS[-1]:
        sums[2 * w] = sums[w] + pltpu.roll(sums[w], w, 0)
        w *= 2
    pos = (i % tiles_per_seq) * tm + lax.broadcasted_iota(jnp.int32, (tm, 1), 0) + 1
    cg = zc.shape[1] // len(POOL_WINDOWS)
    mixed = []
    for g, win in enumerate(POOL_WINDOWS):
        cols = slice(g * cg, (g + 1) * cg)
        cnt = jnp.minimum(pos, win).astype(F32)
        pooled = sums[win][:tm, cols] / cnt - zc[:, cols]
        mixed.append(jnp.dot(pooled.astype(BF16), wp_ref[g], preferred_element_type=F32))
    mixed = jnp.concatenate(mixed, axis=1) * ps_ref[...]

    mix = jnp.concatenate([att, mixed], axis=1).astype(BF16)
    x1 = x_ref[...] + jnp.dot(mix, wo_ref[...], preferred_element_type=F32)
    x1_ref[...] = x1
    hf = _rms(x1, g_ref[...])
    q_ref[...] = jnp.dot(hf.astype(BF16), wq_ref[...], preferred_element_type=F32).astype(BF16)
    hft_ref[...] = hf.T.astype(BF16)


def _mix_out(att, zp, x2, wp_bf, ps, wo_bf, g, wq_bf, seq, tm):
    t, d = x2.shape
    dp = zp.shape[1]
    dq = wq_bf.shape[1]
    row = lambda i: (i, 0)
    const2 = lambda i: (0, 0)
    halo_rows = tm // POOL_HALO
    return pl.pallas_call(
        functools.partial(_mixout_kernel, tiles_per_seq=seq // tm),
        grid=(t // tm,),
        in_specs=[
            pl.BlockSpec((att.shape[0], tm, LANES), lambda i: (0, i, 0)),
            pl.BlockSpec((tm, dp), row),
            pl.BlockSpec((POOL_HALO, dp), lambda i: (jnp.maximum(i * halo_rows - 1, 0), 0)),
            pl.BlockSpec((tm, d), row),
            pl.BlockSpec(wp_bf.shape, lambda i: (0, 0, 0)),
            pl.BlockSpec((1, dp), const2),
            pl.BlockSpec(wo_bf.shape, const2),
            pl.BlockSpec((1, d), const2),
            pl.BlockSpec(wq_bf.shape, const2)],
        out_specs=[pl.BlockSpec((tm, d), row), pl.BlockSpec((tm, dq), row),
                   pl.BlockSpec((d, tm), lambda i: (0, i))],
        out_shape=[jax.ShapeDtypeStruct((t, d), F32), jax.ShapeDtypeStruct((t, dq), BF16),
                   jax.ShapeDtypeStruct((d, t), BF16)],
        compiler_params=_cparams(("parallel",)),
        name="mix_out",
    )(att, zp, zp, x2, wp_bf, ps, wo_bf, g, wq_bf)


def _young_cells():
    return [(a, b) for a in range(PEER_TOPK) for b in range(PEER_TOPK)
            if (a + 1) * (b + 1) <= PEER_TOPK]


def _sort16_pairs():
    n, pairs, p = PEER_TOPK, [], 1
    while p < n:
        k = p
        while k >= 1:
            for j in range(k % p, n - k, 2 * k):
                for i in range(min(k, n - j - k)):
                    if (i + j) // (2 * p) == (i + j + k) // (2 * p):
                        pairs.append((i + j, i + j + k))
            k //= 2
        p *= 2
    return pairs


def _top16_sorted(vals):
    groups = []
    for g in range(0, len(vals), PEER_TOPK):
        x = list(vals[g:g + PEER_TOPK])
        for a, b in _sort16_pairs():
            x[a], x[b] = jnp.maximum(x[a], x[b]), jnp.minimum(x[a], x[b])
        groups.append(x)
    while len(groups) > 1:
        merged = []
        for x, y in zip(groups[0::2], groups[1::2]):
            z = [jnp.maximum(x[i], y[PEER_TOPK - 1 - i]) for i in range(PEER_TOPK)]
            d = PEER_TOPK // 2
            while d >= 1:
                for i in range(PEER_TOPK):
                    if i & d == 0:
                        z[i], z[i + d] = jnp.maximum(z[i], z[i + d]), jnp.minimum(z[i], z[i + d])
                d //= 2
            merged.append(z)
        groups = merged
    return groups[0]


def _bisect(s, v):
    c3 = v[7] > s
    c2 = jnp.where(c3, v[11], v[3]) > s
    c1 = jnp.where(c3, jnp.where(c2, v[13], v[9]), jnp.where(c2, v[5], v[1])) > s
    t0 = jnp.where(c3,
                   jnp.where(c2, jnp.where(c1, v[14], v[12]), jnp.where(c1, v[10], v[8])),
                   jnp.where(c2, jnp.where(c1, v[6], v[4]), jnp.where(c1, v[2], v[0])))
    return c3, c2, c1, t0 > s


def _count_greater(s, v):
    c3, c2, c1, c0 = _bisect(s, v)
    count = (jnp.where(c3, 8.0, 0.0) + jnp.where(c2, 4.0, 0.0)
             + jnp.where(c1, 2.0, 0.0) + jnp.where(c0, 1.0, 0.0))
    return jnp.where(v[15] > s, float(PEER_TOPK), count)


def _table_at_count(s, v, table):
    c3, c2, c1, c0 = _bisect(s, v)

    def pick(lo, bits):
        if not bits:
            return table[lo]
        span = 1 << (len(bits) - 1)
        return jnp.where(bits[0], pick(lo + span, bits[1:]), pick(lo, bits[1:]))

    return jnp.where(v[15] > s, 0.0, pick(0, (c3, c2, c1, c0)))


def _swap_sublanes_with_list(tiles):
    rows = lax.broadcasted_iota(jnp.int32, tiles[0].shape, 0)
    t = list(tiles)
    for d in (4, 2, 1):
        keep = (rows & d) == 0
        for i in range(8):
            if i & d == 0:
                x, y = t[i], t[i + d]
                t[i] = jnp.where(keep, x, pltpu.roll(y, d, 0))
                t[i + d] = jnp.where(keep, pltpu.roll(x, 8 - d, 0), y)
    return t


def _route_kernel(q_ref, sk_ref, rank1_ref, e1_ref, nk_ref, e0_ref,
                  s_scr, rank_scr, val_scr):
    tmr = q_ref.shape[0]
    q = q_ref[...]
    scores = [lax.dot_general(sk_ref[hp], q[:, hp * N_KEYS:(hp + 1) * N_KEYS],
                              (((1,), (1,)), ((), ())), preferred_element_type=F32)
              for hp in range(2 * PEER_HEADS)]
    for p in range(2):
        for kb in range(N_KEYS // 8):
            per_key = _swap_sublanes_with_list(
                [scores[2 * h + p][kb * 8:(kb + 1) * 8, :] for h in range(PEER_HEADS)])
            for i in range(8):
                s_scr[p, kb * 8 + i] = per_key[i]

    flawed = jnp.zeros((PEER_HEADS, tmr), F32)
    for p in range(2):
        s = [s_scr[p, k] for k in range(N_KEYS)]
        v = _top16_sorted(s)
        reach = jnp.zeros((PEER_HEADS, tmr), F32)
        for k in range(N_KEYS):
            reach = reach + jnp.where(v[PEER_TOPK - 1] > s[k], 0.0, 1.0)
        flawed = jnp.maximum(flawed, jnp.abs(reach - float(PEER_TOPK)))
        for r in range(PEER_TOPK):
            val_scr[p, r] = v[r]
            if r:
                flawed = jnp.maximum(flawed, jnp.where(v[r - 1] > v[r], 0.0, 1.0))
    tied = jnp.max(flawed) > 0.0

    @pl.when(tied)
    def _():
        for p in range(2):
            for k in range(N_KEYS):
                rank_scr[p, k] = jnp.zeros((PEER_HEADS, tmr), F32)

            def against(kp, carry):
                other = s_scr[p, kp]
                for k in range(N_KEYS):
                    mine = s_scr[p, k]
                    beats = (other > mine) | ((other == mine) & (kp < k))
                    rank_scr[p, k] = rank_scr[p, k] + jnp.where(beats, 1.0, 0.0)
                return carry

            lax.fori_loop(0, N_KEYS, against, 0)
            for r in range(PEER_TOPK):
                v = jnp.zeros((PEER_HEADS, tmr), F32)
                for k in range(N_KEYS):
                    v = jnp.where(rank_scr[p, k] == float(r), s_scr[p, k], v)
                val_scr[p, r] = v

    cells = _young_cells()
    v0 = [val_scr[0, a] for a in range(PEER_TOPK)]
    v1 = [val_scr[1, b] for b in range(PEER_TOPK)]
    csum = {c: v0[c[0]] + v1[c[1]] for c in cells}
    beaten = {c: jnp.full(csum[c].shape, float((c[0] + 1) * (c[1] + 1) - 1), F32) for c in cells}
    for ix, cx in enumerate(cells):
        for cy in cells[ix + 1:]:
            comparable = (cx[0] <= cy[0] and cx[1] <= cy[1]) or (cy[0] <= cx[0] and cy[1] <= cx[1])
            if comparable:
                continue
            y_wins = jnp.where(csum[cy] > csum[cx], 1.0, 0.0)
            beaten[cx] = beaten[cx] + y_wins
            beaten[cy] = beaten[cy] + (1.0 - y_wins)
    top = csum[(0, 0)]
    zsum = jnp.zeros_like(top)
    ncol = [jnp.zeros_like(top) for _ in range(PEER_TOPK)]
    for c in cells:
        chosen = beaten[c] < float(PEER_TOPK)
        zsum = zsum + jnp.where(chosen, jnp.exp(csum[c] - top), 0.0)
        ncol[c[0]] = ncol[c[0]] + jnp.where(chosen, 1.0, 0.0)
    inv_z = 1.0 / zsum

    def emit(partners_of_key, rank_of_key):
        for k in range(N_KEYS):
            nk_ref[k] = partners_of_key(k)
            e0_ref[k] = jnp.exp(s_scr[0, k] - v0[0]) * inv_z
        for kb in range(N_KEYS // PACK):
            rank_rows, e1_rows = [], []
            for half in range(PACK // 8):
                keys = range(kb * PACK + half * 8, kb * PACK + half * 8 + 8)
                rank_rows.append(_swap_sublanes_with_list([rank_of_key(k) for k in keys]))
                e1_rows.append(_swap_sublanes_with_list([jnp.exp(s_scr[1, k] - v1[0]) for k in keys]))
            for h in range(PEER_HEADS):
                rows = slice(kb * PACK, (kb + 1) * PACK)
                rank1_ref[h, rows, :] = jnp.concatenate([part[h] for part in rank_rows], axis=0).astype(BF16)
                e1_ref[h, rows, :] = jnp.concatenate([part[h] for part in e1_rows], axis=0).astype(BF16)

    @pl.when(jnp.logical_not(tied))
    def _():
        emit(lambda k: _table_at_count(s_scr[0, k], v0, ncol),
             lambda k: _count_greater(s_scr[1, k], v1))

    @pl.when(tied)
    def _():
        def partners(k):
            rank0 = rank_scr[0, k]
            nk = jnp.zeros((PEER_HEADS, tmr), F32)
            for a in range(PEER_TOPK):
                nk = jnp.where(rank0 == float(a), ncol[a], nk)
            return nk

        emit(partners, lambda k: rank_scr[1, k])


def _route(q, sk_bf, tmr):
    t, d = q.shape
    sk2 = sk_bf.reshape(2 * PEER_HEADS, N_KEYS, sk_bf.shape[-1])
    dense = jax.ShapeDtypeStruct((PEER_HEADS, N_KEYS, t), BF16)
    dense_spec = pl.BlockSpec((PEER_HEADS, N_KEYS, tmr), lambda i: (0, 0, i))
    rows = jax.ShapeDtypeStruct((N_KEYS, PEER_HEADS, t), F32)
    rows_spec = pl.BlockSpec((N_KEYS, PEER_HEADS, tmr), lambda i: (0, 0, i))
    return pl.pallas_call(
        _route_kernel,
        grid=(t // tmr,),
        in_specs=[pl.BlockSpec((tmr, d), lambda i: (i, 0)),
                  pl.BlockSpec(sk2.shape, lambda i: (0, 0, 0))],
        out_specs=[dense_spec, dense_spec, rows_spec, rows_spec],
        out_shape=[dense, dense, rows, rows],
        scratch_shapes=[pltpu.VMEM((2, N_KEYS, PEER_HEADS, tmr), F32),
                        pltpu.VMEM((2, N_KEYS, PEER_HEADS, tmr), F32),
                        pltpu.VMEM((2, PEER_TOPK, PEER_HEADS, tmr), F32)],
        compiler_params=_cparams(("parallel",)),
        name="route",
    )(q, sk2)


PACK = 16
GATE_LANES = 256


def _gate_rows(row_ref, h, r, lanes):
    row = row_ref[r, h:h + 1, lanes]
    half = jnp.broadcast_to(row, (PACK // 2, row.shape[1]))
    return jnp.concatenate([half, half], axis=0).astype(BF16)


def _peer_gate_chunk(s_ref, w_ref, rank1_ref, e1_ref, nk_ref, e0_ref):
    ec, tm = s_ref.shape
    subs = N_KEYS // PACK
    for j in range(ec // N_KEYS):
        for g in range(tm // GATE_LANES):
            lanes = slice(g * GATE_LANES, (g + 1) * GATE_LANES)
            gate = [jnp.zeros((PACK, GATE_LANES), BF16) for _ in range(subs)]
            for h in range(PEER_HEADS):
                n_rows = _gate_rows(nk_ref, h, j, lanes)
                e0_rows = _gate_rows(e0_ref, h, j, lanes)
                for k in range(subs):
                    rows = slice(k * PACK, (k + 1) * PACK)
                    chosen = rank1_ref[h, rows, lanes] < n_rows
                    picked = jnp.where(chosen, e1_ref[h, rows, lanes], jnp.zeros((), BF16))
                    gate[k] = gate[k] + picked * e0_rows
            for k in range(subs):
                rows = slice(j * N_KEYS + k * PACK, j * N_KEYS + (k + 1) * PACK)
                s = s_ref[rows, lanes].astype(BF16)
                act = 0.5 * s * (1.0 + lax.erf(s * (0.5 ** 0.5)))
                w_ref[rows, lanes] = act * gate[k]


def _peer_kernel(u_ref, hft_ref, vt_ref, rank1_ref, e1_ref, nk_ref, e0_ref, x1_ref, g_ref, o_ref,
                 s_scr, w_scr, yt_scr, *, final_norm):
    c = pl.program_id(1)

    @pl.when(c == 0)
    def _():
        yt_scr[...] = jnp.zeros_like(yt_scr)

    s_scr[...] = jnp.dot(u_ref[...], hft_ref[...], preferred_element_type=F32)
    _peer_gate_chunk(s_scr, w_scr, rank1_ref, e1_ref, nk_ref, e0_ref)
    yt_scr[...] += jnp.dot(vt_ref[...], w_scr[...], preferred_element_type=F32)

    @pl.when(c == pl.num_programs(1) - 1)
    def _():
        x2 = x1_ref[...] + yt_scr[...].T
        o_ref[...] = _rms(x2, g_ref[...]) if final_norm else x2


def _peer(u_bf, hft, vt_bf, rank1, e1, nk, e0, x1, g, final_norm, tm, ec):
    n_exp, d = u_bf.shape
    t = hft.shape[1]
    rspec = pl.BlockSpec((PEER_HEADS, N_KEYS, tm), lambda i, c: (0, 0, i))
    rows = pl.BlockSpec((ec // N_KEYS, PEER_HEADS, tm), lambda i, c: (c, 0, i))
    return pl.pallas_call(
        functools.partial(_peer_kernel, final_norm=final_norm),
        grid=(t // tm, n_exp // ec),
        in_specs=[pl.BlockSpec((ec, d), lambda i, c: (c, 0)),
                  pl.BlockSpec((d, tm), lambda i, c: (0, i)),
                  pl.BlockSpec((d, ec), lambda i, c: (0, c)),
                  rspec, rspec, rows, rows,
                  pl.BlockSpec((tm, d), lambda i, c: (i, 0)),
                  pl.BlockSpec((1, d), lambda i, c: (0, 0))],
        out_specs=pl.BlockSpec((tm, d), lambda i, c: (i, 0)),
        out_shape=jax.ShapeDtypeStruct((t, d), F32),
        scratch_shapes=[pltpu.VMEM((ec, tm), F32), pltpu.VMEM((ec, tm), BF16),
                        pltpu.VMEM((d, tm), F32)],
        compiler_params=_cparams(("parallel", "arbitrary")),
        name="peer",
    )(u_bf, hft, vt_bf, rank1, e1, nk, e0, x1, g)


def _pick(total, want):
    tile = min(total, want)
    assert total % tile == 0, (total, tile)
    return tile


def kernel(x, norm_mix, w_in, w_pool, pool_scale, w_out, norm_ffn, w_query, sub_keys,
           expert_u, expert_v, norm_final):
    b, s, d = x.shape
    t = b * s
    depth = norm_mix.shape[0]
    d_pool = w_pool.shape[1] * w_pool.shape[2]
    d_qkv = w_in.shape[2] - d_pool
    d_attn = d_qkv // 3
    assert d_attn == w_out.shape[1] - d_pool and d_attn % LANES == 0
    assert sub_keys.shape[1:] == (PEER_HEADS, 2, N_KEYS, N_KEYS)
    assert expert_u.shape[1] == N_KEYS * N_KEYS
    assert s % (DILATED_PATTERNS[-1][1] * WBLK) == 0

    tm = _pick(s, 1024)
    tm_route = _pick(s, 128)
    tm_peer = _pick(s, 1024)
    ec = 1024

    x2 = x.reshape(t, d)
    for layer in range(depth):
        qkv, zp = _in_proj(x2, norm_mix[layer][None], w_in[layer].astype(BF16), d_qkv, tm)
        att = _attention(qkv, b, s)
        x1, q, hft = _mix_out(att, zp, x2, w_pool[layer].astype(BF16),
                              pool_scale[layer][None], w_out[layer].astype(BF16),
                              norm_ffn[layer][None], w_query[layer].astype(BF16), s, tm)
        rank1, e1, nk, e0 = _route(q, sub_keys[layer].astype(BF16), tm_route)
        last = layer + 1 == depth
        x2 = _peer(expert_u[layer].astype(BF16), hft, expert_v[layer].astype(BF16).T,
                   rank1, e1, nk, e0, x1, norm_final[None], last, tm_peer, ec)
    return x2.reshape(b, s, d)
```
